```python
import math
import jax, jax.numpy as jnp
from jax import lax
import numpy as np

D_MODEL = 4096
BATCH = 4
SEQ = 2048
DEPTH = 2

MIX_WIDTH = D_MODEL
ATTN_WIDTH = MIX_WIDTH // 2
HYENA_WIDTH = MIX_WIDTH - ATTN_WIDTH
HEAD_DIM = 128
N_HEADS = ATTN_WIDTH // HEAD_DIM
DILATED_CONFIGS = ((128, 1), (512, 4), (2048, 16))
ROPE_THETA = 10000.0
HYENA_ORDER = 2
HYENA_GROUP = 128
N_HYENA_GROUPS = HYENA_WIDTH // HYENA_GROUP
SHORT_CONV = 3
FILTER_EMB_DIM = 33
FILTER_HIDDEN = 64
DECAY_FAST = 0.3
DECAY_SLOW = 1.5
DECAY_TARGET = 1e-2
FFN_HIDDEN = 256 * (-(-(8 * D_MODEL) // (3 * 256)))
IN_COLS = 3 * ATTN_WIDTH + (HYENA_ORDER + 1) * HYENA_WIDTH
RMS_EPS = 1e-6
NEG_INF = -1e30

kernel_name = "hymba_dilated_attn_hyena_macaron"


def _rms(x, g, eps=RMS_EPS):
    xf = x.astype(jnp.float32)
    y = xf * lax.rsqrt(jnp.mean(xf * xf, axis=-1, keepdims=True) + eps)
    return y * g.astype(jnp.float32)


def _group_rms(x, g, group):
    shp = x.shape
    xg = x.reshape(shp[:-1] + (shp[-1] // group, group))
    return _rms(xg, g.reshape(shp[-1] // group, group)).reshape(shp)


def _swiglu(h, wg, wu, wd):
    return (jax.nn.silu(h @ wg) * (h @ wu)) @ wd


def _rotary(x, pos):
    half = x.shape[-1] // 2
    inv = ROPE_THETA ** (-jnp.arange(half, dtype=jnp.float32) / half)
    ang = pos[:, None] * inv[None, :]
    cos = jnp.cos(ang)[None, :, None, :]
    sin = jnp.sin(ang)[None, :, None, :]
    x1, x2 = x[..., :half], x[..., half:]
    return jnp.concatenate([x1 * cos - x2 * sin, x2 * cos + x1 * sin], axis=-1)


def _dilated_window_attention(q, k, v, window, dil):
    B, S, H, Dh = q.shape
    half = window // (2 * dil)
    blk = half
    unit = dil * blk
    Sp = -(-S // unit) * unit
    L = Sp // dil
    nb = L // blk
    pad = ((0, 0), (0, Sp - S), (0, 0), (0, 0))

    def to_blocks(a):
        a = jnp.pad(a, pad)
        return a.reshape(B, L, dil, H, Dh).transpose(0, 2, 1, 3, 4).reshape(B, dil, nb, blk, H, Dh)

    def with_neighbours(a):
        ap = jnp.pad(a, [(0, 0), (0, 0), (1, 1)] + [(0, 0)] * (a.ndim - 3))
        return jnp.concatenate([ap[:, :, :-2], ap[:, :, 1:-1], ap[:, :, 2:]], axis=3)

    qb = to_blocks(q)
    kb = with_neighbours(to_blocks(k))
    vb = with_neighbours(to_blocks(v))
    valid = (jnp.arange(Sp) < S).reshape(L, dil).T.reshape(1, dil, nb, blk)
    kvalid = with_neighbours(valid)
    rel = jnp.arange(3 * blk)[None, :] - blk - jnp.arange(blk)[:, None]
    band = jnp.abs(rel) <= half
    mask = band[None, None, None, None] & kvalid[:, :, :, None, None, :]

    s = jnp.einsum('brnqhd,brnkhd->brnhqk', qb, kb) / math.sqrt(Dh)
    s = jnp.where(mask, s, NEG_INF)
    m = jnp.max(s, axis=-1, keepdims=True)
    p = jnp.exp(s - m)
    den = jnp.sum(p, axis=-1, keepdims=True)
    o = jnp.einsum('brnhqk,brnkhd->brnqhd', p / den, vb)
    lse = (m + jnp.log(den))[..., 0].transpose(0, 1, 2, 4, 3)
    o = o.reshape(B, dil, L, H, Dh).transpose(0, 2, 1, 3, 4).reshape(B, Sp, H, Dh)[:, :S]
    lse = lse.reshape(B, dil, L, H).transpose(0, 2, 1, 3).reshape(B, Sp, H)[:, :S]
    return o, lse


def _dilated_attention_mixer(q, k, v):
    outs, lses = [], []
    for window, dil in DILATED_CONFIGS:
        o, l = _dilated_window_attention(q, k, v, window, dil)
        outs.append(o)
        lses.append(l)
    w = jax.nn.softmax(jnp.stack(lses, axis=0), axis=0)
    return jnp.sum(w[..., None] * jnp.stack(outs, axis=0), axis=0)


def _hyena_filters(L, w1, b1, w2, b2, freq, w3):
    f32 = jnp.float32
    t = jnp.linspace(0.0, 1.0, L, dtype=f32)[:, None]
    bands = (FILTER_EMB_DIM - 1) // 2
    f = jnp.linspace(1e-4, bands - 1, bands, dtype=f32)[None, :]
    wpos = (2.0 * math.pi) * jnp.arange(L, dtype=f32)[:, None] / L
    emb = jnp.concatenate([t, jnp.cos(f * wpos), -jnp.sin(f * wpos)], axis=-1)
    freq = freq.astype(f32)
    h = jnp.sin(freq[0] * (emb @ w1.astype(f32) + b1.astype(f32)))
    h = jnp.sin(freq[1] * (h @ w2.astype(f32) + b2.astype(f32)))
    h = h @ w3.astype(f32)
    deltas = jnp.abs(jnp.linspace(math.log(DECAY_FAST) / DECAY_TARGET,
                                  math.log(DECAY_SLOW) / DECAY_TARGET, HYENA_WIDTH, dtype=f32))
    decay = jnp.exp(-t * deltas[None, :])
    return h.reshape(L, HYENA_ORDER, 2, HYENA_WIDTH) * decay[:, None, None, :]


def _bidirectional_long_conv(z, h_fwd, h_bwd, bias):
    L, C = h_fwd.shape
    k2 = jnp.concatenate([h_fwd, jnp.zeros((1, C), h_fwd.dtype), h_bwd[1:][::-1]], axis=0)
    Z = jnp.fft.rfft(z, n=2 * L, axis=1)
    K = jnp.fft.rfft(k2, axis=0)
    y = jnp.fft.irfft(Z * K[None], n=2 * L, axis=1)[:, :L]
    return y + bias.astype(jnp.float32) * z


def _hyena_mixer(u, conv_w, conv_b, w1, b1, w2, b2, freq, w3, filt_bias):
    S = u.shape[1]
    up = jnp.pad(u, ((0, 0), (1, 1), (0, 0)))
    u = (conv_w[0] * up[:, :-2] + conv_w[1] * up[:, 1:-1] + conv_w[2] * up[:, 2:] + conv_b).astype(jnp.float32)
    z = u[..., :HYENA_WIDTH]
    gates = [u[..., (n + 1) * HYENA_WIDTH:(n + 2) * HYENA_WIDTH] for n in range(HYENA_ORDER)]
    filt = _hyena_filters(S, w1, b1, w2, b2, freq, w3)
    for n in range(HYENA_ORDER):
        z = gates[n] * _bidirectional_long_conv(z, filt[:, n, 0], filt[:, n, 1], filt_bias[n])
    return z


def setup_inputs(seed: int = 0) -> dict:
    key = jax.random.key(seed)
    ks = jax.random.split(key, 24)
    f32 = jnp.float32
    D, F, A, C = D_MODEL, FFN_HIDDEN, ATTN_WIDTH, HYENA_WIDTH
    nrm = lambda k, shape, scale: jax.random.normal(k, shape, f32) * scale
    return {
        "x": nrm(ks[0], (BATCH, SEQ, D), 1.0),
        "ffn_norm": 1.0 + nrm(ks[1], (DEPTH, 2, D), 0.02),
        "ffn_w_gate": nrm(ks[2], (DEPTH, 2, D, F), D ** -0.5),
        "ffn_w_up": nrm(ks[3], (DEPTH, 2, D, F), D ** -0.5),
        "ffn_w_down": nrm(ks[4], (DEPTH, 2, F, D), F ** -0.5),
        "mix_norm": 1.0 + nrm(ks[5], (DEPTH, D), 0.02),
        "w_in": nrm(ks[6], (DEPTH, D, IN_COLS), D ** -0.5),
        "q_norm": 1.0 + nrm(ks[7], (DEPTH, HEAD_DIM), 0.02),
        "k_norm": 1.0 + nrm(ks[8], (DEPTH, HEAD_DIM), 0.02),
        "conv_w": nrm(ks[9], (DEPTH, SHORT_CONV, (HYENA_ORDER + 1) * C), SHORT_CONV ** -0.5),
        "conv_b": nrm(ks[10], (DEPTH, (HYENA_ORDER + 1) * C), 0.02),
        "filt_w1": nrm(ks[11], (DEPTH, FILTER_EMB_DIM, FILTER_HIDDEN), FILTER_EMB_DIM ** -0.5),
        "filt_b1": nrm(ks[12], (DEPTH, FILTER_HIDDEN), 0.02),
        "filt_w2": nrm(ks[13], (DEPTH, FILTER_HIDDEN, FILTER_HIDDEN), FILTER_HIDDEN ** -0.5),
        "filt_b2": nrm(ks[14], (DEPTH, FILTER_HIDDEN), 0.02),
        "filt_freq": 1.0 + nrm(ks[15], (DEPTH, 2, FILTER_HIDDEN), 0.02),
        "filt_w3": nrm(ks[16], (DEPTH, FILTER_HIDDEN, HYENA_ORDER * 2 * C), FILTER_HIDDEN ** -0.5),
        "filt_bias": nrm(ks[17], (DEPTH, HYENA_ORDER, C), 1.0),
        "attn_out_norm": 1.0 + nrm(ks[18], (DEPTH, A), 0.02),
        "hyena_out_norm": 1.0 + nrm(ks[19], (DEPTH, C), 0.02),
        "w_out": nrm(ks[20], (DEPTH, MIX_WIDTH, D), MIX_WIDTH ** -0.5),
    }


def reference(x, ffn_norm, ffn_w_gate, ffn_w_up, ffn_w_down, mix_norm, w_in, q_norm, k_norm,
              conv_w, conv_b, filt_w1, filt_b1, filt_w2, filt_b2, filt_freq, filt_w3, filt_bias,
              attn_out_norm, hyena_out_norm, w_out):
    B, S, _ = x.shape
    pos = jnp.arange(S, dtype=jnp.float32)
    A = ATTN_WIDTH
    for l in range(DEPTH):
        h = _rms(x, ffn_norm[l, 0]).astype(x.dtype)
        x = x + 0.5 * _swiglu(h, ffn_w_gate[l, 0], ffn_w_up[l, 0], ffn_w_down[l, 0])

        h = _rms(x, mix_norm[l]).astype(x.dtype)
        proj = h @ w_in[l]
        q = proj[..., :A].reshape(B, S, N_HEADS, HEAD_DIM)
        k = proj[..., A:2 * A].reshape(B, S, N_HEADS, HEAD_DIM)
        v = proj[..., 2 * A:3 * A].reshape(B, S, N_HEADS, HEAD_DIM).astype(jnp.float32)
        q = _rotary(_rms(q, q_norm[l]), pos)
        k = _rotary(_rms(k, k_norm[l]), pos)
        attn = _dilated_attention_mixer(q, k, v).reshape(B, S, A)

        hy = _hyena_mixer(proj[..., 3 * A:], conv_w[l], conv_b[l], filt_w1[l], filt_b1[l],
                          filt_w2[l], filt_b2[l], filt_freq[l], filt_w3[l], filt_bias[l])

        merged = jnp.concatenate([_group_rms(attn, attn_out_norm[l], HEAD_DIM),
                                  _group_rms(hy, hyena_out_norm[l], HYENA_GROUP)], axis=-1)
        x = x + merged.astype(x.dtype) @ w_out[l]

        h = _rms(x, ffn_norm[l, 1]).astype(x.dtype)
        x = x + 0.5 * _swiglu(h, ffn_w_gate[l, 1], ffn_w_up[l, 1], ffn_w_down[l, 1])
    return x
```

```python
import functools
import math

import numpy as np
import jax
import jax.numpy as jnp
from jax import lax
from jax.experimental import pallas as pl
from jax.experimental.pallas import tpu as pltpu

D_MODEL = 4096
BATCH = 4
SEQ = 2048
DEPTH = 2
ATTN_WIDTH = D_MODEL // 2
HYENA_WIDTH = D_MODEL - ATTN_WIDTH
HEAD_DIM = 128
N_HEADS = ATTN_WIDTH // HEAD_DIM
DILATED_CONFIGS = ((128, 1), (512, 4), (2048, 16))
ROPE_THETA = 10000.0
HYENA_ORDER = 2
HYENA_GROUP = 128
FILTER_EMB_DIM = 33
FILTER_HIDDEN = 64
DECAY_FAST = 0.3
DECAY_SLOW = 1.5
DECAY_TARGET = 1e-2
FFN_HIDDEN = 256 * (-(-(8 * D_MODEL) // (3 * 256)))
IN_COLS = 3 * ATTN_WIDTH + (HYENA_ORDER + 1) * HYENA_WIDTH
RMS_EPS = 1e-6
NEG_INF = -1e30

F32 = jnp.float32
BF16 = jnp.bfloat16

V7X_VMEM_LIMIT_BYTES = 56 * 1024 * 1024
LANES = 128
FFN_TILE = 1024
FFN_PAD = -(-FFN_HIDDEN // FFN_TILE) * FFN_TILE
ATTN_TQ = 256
ATTN_REACH = max((w // (2 * d)) * d for w, d in DILATED_CONFIGS)
DFT_N = 2 * SEQ


def _params(*sem):
    return pltpu.CompilerParams(dimension_semantics=sem, vmem_limit_bytes=V7X_VMEM_LIMIT_BYTES)


def _dot(a, b):
    return jnp.dot(a, b, preferred_element_type=F32)


def _rmsnorm_body(x_ref, g_ref, o_ref):
    x = x_ref[...]
    ms = jnp.mean(x * x, axis=-1, keepdims=True)
    o_ref[...] = (x * lax.rsqrt(ms + RMS_EPS) * g_ref[...]).astype(o_ref.dtype)


def _rmsnorm(x, g, tm=512):
    m, d = x.shape
    return pl.pallas_call(
        _rmsnorm_body,
        grid=(m // tm,),
        in_specs=[pl.BlockSpec((tm, d), lambda i: (i, 0)),
                  pl.BlockSpec((1, d), lambda i: (0, 0))],
        out_specs=pl.BlockSpec((tm, d), lambda i: (i, 0)),
        out_shape=jax.ShapeDtypeStruct((m, d), BF16),
        compiler_params=_params("parallel"),
        name="rmsnorm",
    )(x, g.reshape(1, d))


def _swiglu_body(a_ref, wg_ref, wu_ref, o_ref):
    a = a_ref[...]
    g = _dot(a, wg_ref[...])
    u = _dot(a, wu_ref[...])
    o_ref[...] = (g * jax.nn.sigmoid(g) * u).astype(o_ref.dtype)


def _ffn_up(h, wg, wu, tm=1024, tn=512):
    m, k = h.shape
    n = wg.shape[1]
    return pl.pallas_call(
        _swiglu_body,
        grid=(m // tm, n // tn),
        in_specs=[pl.BlockSpec((tm, k), lambda i, j: (i, 0)),
                  pl.BlockSpec((k, tn), lambda i, j: (0, j)),
                  pl.BlockSpec((k, tn), lambda i, j: (0, j))],
        out_specs=pl.BlockSpec((tm, tn), lambda i, j: (i, j)),
        out_shape=jax.ShapeDtypeStruct((m, n), BF16),
        compiler_params=_params("parallel", "arbitrary"),
        name="ffn_up",
    )(h, wg, wu)


def _mm_body(a_ref, w_ref, o_ref):
    o_ref[...] = _dot(a_ref[...], w_ref[...]).astype(o_ref.dtype)


def _mm(a, w, out_dtype, tm=1024, tn=512):
    m, k = a.shape
    n = w.shape[1]
    return pl.pallas_call(
        _mm_body,
        grid=(m // tm, n // tn),
        in_specs=[pl.BlockSpec((tm, k), lambda i, j: (i, 0)),
                  pl.BlockSpec((k, tn), lambda i, j: (0, j))],
        out_specs=pl.BlockSpec((tm, tn), lambda i, j: (i, j)),
        out_shape=jax.ShapeDtypeStruct((m, n), out_dtype),
        compiler_params=_params("parallel", "arbitrary"),
        name="mm",
    )(a, w)


def _mm_res_body(a_ref, w_ref, r_ref, o_ref, *, scale):
    o_ref[...] = r_ref[...] + scale * _dot(a_ref[...], w_ref[...])


def _mm_res(a, w, res, scale, tm=1024, tn=256, single_buffer_a=False):
    m, k = a.shape
    n = w.shape[1]
    a_kwargs = dict(pipeline_mode=pl.Buffered(1)) if single_buffer_a else {}
    return pl.pallas_call(
        functools.partial(_mm_res_body, scale=scale),
        grid=(m // tm, n // tn),
        in_specs=[pl.BlockSpec((tm, k), lambda i, j: (i, 0), **a_kwargs),
                  pl.BlockSpec((k, tn), lambda i, j: (0, j)),
                  pl.BlockSpec((tm, tn), lambda i, j: (i, j))],
        out_specs=pl.BlockSpec((tm, tn), lambda i, j: (i, j)),
        out_shape=jax.ShapeDtypeStruct((m, n), F32),
        compiler_params=_params("parallel", "arbitrary"),
        name="mm_res",
    )(a, w, res)


def _attn_bias_table():
    il = np.arange(ATTN_TQ)[:, None]
    c = np.arange(ATTN_TQ + 2 * ATTN_REACH)[None, :]
    d = il + ATTN_REACH - c
    mult = np.zeros(d.shape, np.int64)
    for window, dil in DILATED_CONFIGS:
        half = window // (2 * dil)
        mult += ((d % dil) == 0) & (np.abs(d) <= half * dil)
    return np.where(mult > 0, np.log(np.maximum(mult, 1)), NEG_INF).astype(np.float32)


def _attn_body(q_ref, k_ref, v_ref, gq_ref, gk_ref, cos_ref, sin_ref, bias_ref, go_ref,
               o_ref, qs_ref, ks_ref, vs_ref):
    seq = q_ref.shape[1]
    cos = cos_ref[...]
    sin = sin_ref[...]

    def norm_rot(x, g):
        y = x * lax.rsqrt(jnp.mean(x * x, axis=-1, keepdims=True) + RMS_EPS) * g
        return y * cos + pltpu.roll(y, HEAD_DIM // 2, axis=1) * sin

    qs_ref[...] = (norm_rot(q_ref[0], gq_ref[...]) * (1.0 / math.sqrt(HEAD_DIM))).astype(BF16)
    ks_ref[...] = norm_rot(k_ref[0], gk_ref[...]).astype(BF16)
    vs_ref[...] = v_ref[0].astype(BF16)

    for t in range(seq // ATTN_TQ):
        q0 = t * ATTN_TQ
        lo = max(0, q0 - ATTN_REACH)
        hi = min(seq, q0 + ATTN_TQ + ATTN_REACH)
        c0 = lo - (q0 - ATTN_REACH)
        s = lax.dot_general(qs_ref[q0:q0 + ATTN_TQ, :], ks_ref[lo:hi, :],
                            (((1,), (1,)), ((), ())), preferred_element_type=F32)
        s = s + bias_ref[:, c0:c0 + (hi - lo)]
        m = jnp.max(s, axis=-1, keepdims=True)
        p = jnp.exp(s - m)
        den = jnp.sum(p, axis=-1, keepdims=True)
        o = _dot(p.astype(BF16), vs_ref[lo:hi, :]) / den
        o = o * lax.rsqrt(jnp.mean(o * o, axis=-1, keepdims=True) + RMS_EPS) * go_ref[...]
        o_ref[0, q0:q0 + ATTN_TQ, :] = o.astype(o_ref.dtype)


def _attention(proj, gq, gk, cos, sin_signed, g_out):
    b, s, _ = proj.shape
    hd = HEAD_DIM
    bias = jnp.asarray(_attn_bias_table())
    head = lambda off: pl.BlockSpec((1, s, hd), lambda i, h: (i, 0, off + h))
    const2 = lambda shape: pl.BlockSpec(shape, lambda i, h: (0, 0))
    return pl.pallas_call(
        _attn_body,
        grid=(b, N_HEADS),
        in_specs=[head(0), head(N_HEADS), head(2 * N_HEADS),
                  const2((1, hd)), const2((1, hd)),
                  const2((s, hd)), const2((s, hd)),
                  const2(bias.shape),
                  pl.BlockSpec((1, hd), lambda i, h: (0, h))],
        out_specs=pl.BlockSpec((1, s, hd), lambda i, h: (i, 0, h)),
        out_shape=jax.ShapeDtypeStruct((b, s, ATTN_WIDTH), BF16),
        scratch_shapes=[pltpu.VMEM((s, hd), BF16)] * 3,
        compiler_params=_params("parallel", "arbitrary"),
        name="attention",
    )(proj, proj, proj, gq.reshape(1, hd), gk.reshape(1, hd), cos, sin_signed, bias,
      g_out.reshape(1, ATTN_WIDTH))


def _shortconv_body(u_ref, w_ref, b_ref, o_ref):
    u = u_ref[0]
    seq = u.shape[0]
    row = lax.broadcasted_iota(jnp.int32, u.shape, 0)
    prev = jnp.where(row == 0, 0.0, pltpu.roll(u, 1, axis=0))
    nxt = jnp.where(row == seq - 1, 0.0, pltpu.roll(u, seq - 1, axis=0))
    o_ref[0] = w_ref[0:1, :] * prev + w_ref[1:2, :] * u + w_ref[2:3, :] * nxt + b_ref[...]


def _shortconv(proj, conv_w, conv_b, tc=512):
    b, s, _ = proj.shape
    width = conv_w.shape[1]
    off = (IN_COLS - width) // tc
    return pl.pallas_call(
        _shortconv_body,
        grid=(b, width // tc),
        in_specs=[pl.BlockSpec((1, s, tc), lambda i, j: (i, 0, off + j)),
                  pl.BlockSpec((3, tc), lambda i, j: (0, j)),
                  pl.BlockSpec((1, tc), lambda i, j: (0, j))],
        out_specs=pl.BlockSpec((1, s, tc), lambda i, j: (i, 0, j)),
        out_shape=jax.ShapeDtypeStruct((b, s, width), F32),
        compiler_params=_params("parallel", "arbitrary"),
        name="shortconv",
    )(proj, conv_w, conv_b.reshape(1, width))


def _filt_hidden_body(emb_ref, w1_ref, b1_ref, w2_ref, b2_ref, fr_ref, o_ref):
    hp = lax.Precision.HIGHEST
    h = jnp.dot(emb_ref[...], w1_ref[...], precision=hp, preferred_element_type=F32)
    h = jnp.sin(fr_ref[0:1, :] * (h + b1_ref[...]))
    h = jnp.dot(h, w2_ref[...], precision=hp, preferred_element_type=F32)
    o_ref[...] = jnp.sin(fr_ref[1:2, :] * (h + b2_ref[...]))


def _filt_out_body(h_ref, w3_ref, t_ref, delta_ref, o_ref):
    h = jnp.dot(h_ref[...], w3_ref[...], precision=lax.Precision.HIGHEST,
                preferred_element_type=F32)
    o_ref[...] = h * jnp.exp(-t_ref[...] * delta_ref[...])


def _pad_to(a, shape):
    return jnp.pad(a, [(0, t - s) for s, t in zip(a.shape, shape)])


def _hyena_filters(w1, b1, w2, b2, freq, w3, tn=512):
    seq = SEQ
    t = jnp.linspace(0.0, 1.0, seq, dtype=F32)[:, None]
    bands = (FILTER_EMB_DIM - 1) // 2
    f = jnp.linspace(1e-4, bands - 1, bands, dtype=F32)[None, :]
    wpos = (2.0 * math.pi) * jnp.arange(seq, dtype=F32)[:, None] / seq
    emb = jnp.concatenate([t, jnp.cos(f * wpos), -jnp.sin(f * wpos)], axis=-1)
    deltas = jnp.abs(jnp.linspace(math.log(DECAY_FAST) / DECAY_TARGET,
                                  math.log(DECAY_SLOW) / DECAY_TARGET, HYENA_WIDTH, dtype=F32))
    ncol = w3.shape[1]
    deltas = jnp.tile(deltas, ncol // HYENA_WIDTH).reshape(1, ncol)
    hid = LANES
    hidden = pl.pallas_call(
        _filt_hidden_body,
        out_shape=jax.ShapeDtypeStruct((seq, hid), F32),
        name="filt_hidden",
    )(_pad_to(emb, (seq, hid)), _pad_to(w1, (hid, hid)), _pad_to(b1[None], (1, hid)),
      _pad_to(w2, (hid, hid)), _pad_to(b2[None], (1, hid)), _pad_to(freq, (2, hid)))
    return pl.pallas_call(
        _filt_out_body,
        grid=(ncol // tn,),
        in_specs=[pl.BlockSpec((seq, hid), lambda j: (0, 0)),
                  pl.BlockSpec((hid, tn), lambda j: (0, j)),
                  pl.BlockSpec((seq, 1), lambda j: (0, 0)),
                  pl.BlockSpec((1, tn), lambda j: (0, j))],
        out_specs=pl.BlockSpec((seq, tn), lambda j: (0, j)),
        out_shape=jax.ShapeDtypeStruct((seq, ncol), F32),
        compiler_params=_params("parallel"),
        name="filt_out",
    )(hidden, _pad_to(w3, (hid, ncol)), t, deltas)


def _dft_matrices():
    n = jnp.arange(SEQ, dtype=jnp.int32)
    prod = (n[:, None] * n[None, :]) & (DFT_N - 1)
    ang = prod.astype(F32) * (2.0 * math.pi / DFT_N)
    c = jnp.cos(ang)
    s = jnp.sin(ang)
    alt = jnp.where(n % 2 == 0, 1.0, -1.0).astype(F32)
    first_row = (n == 0)[:, None]
    first_col = (n == 0)[None, :]
    cm = c
    sm = jnp.where(first_row, alt[None, :], -s)
    ic = c * jnp.where(first_col, 1.0 / DFT_N, 2.0 / DFT_N)
    im = jnp.where(first_col, alt[:, None] / DFT_N, -s * (2.0 / DFT_N))
    return cm.astype(BF16), sm.astype(BF16), ic.astype(BF16), im.astype(BF16)


def _kspec_body(cm_ref, sm_ref, hf_ref, hb_ref, hb0_ref, kre_ref, kim_ref):
    cm = cm_ref[...]
    sm = sm_ref[...]
    hf = hf_ref[...].astype(BF16)
    hb = hb_ref[...].astype(BF16)
    hb0 = hb0_ref[0:1, :]
    kre_ref[...] = _dot(cm, hf) + _dot(cm, hb) - hb0
    fim = _dot(sm, hf)
    bim = _dot(sm, hb)
    kim_ref[...] = fim - bim

    @pl.when(pl.program_id(0) == 0)
    def _():
        row = lax.broadcasted_iota(jnp.int32, (8, fim.shape[1]), 0)
        kim_ref[0:8, :] = jnp.where(row == 0, fim[0:8] + bim[0:8] - hb0, fim[0:8] - bim[0:8])


def _kspec(cm, sm, filt, tf=1024, tn=512):
    c = HYENA_WIDTH
    nc = c // tn
    nbins = cm.shape[0]
    out = jax.ShapeDtypeStruct((nbins, HYENA_ORDER * c), F32)
    return pl.pallas_call(
        _kspec_body,
        grid=(nbins // tf, HYENA_ORDER, nc),
        in_specs=[pl.BlockSpec((tf, SEQ), lambda f, o, j: (f, 0)),
                  pl.BlockSpec((tf, SEQ), lambda f, o, j: (f, 0)),
                  pl.BlockSpec((SEQ, tn), lambda f, o, j: (0, o * 2 * nc + j)),
                  pl.BlockSpec((SEQ, tn), lambda f, o, j: (0, o * 2 * nc + nc + j)),
                  pl.BlockSpec((8, tn), lambda f, o, j: (0, o * 2 * nc + nc + j))],
        out_specs=[pl.BlockSpec((tf, tn), lambda f, o, j: (f, o * nc + j))] * 2,
        out_shape=[out, out],
        compiler_params=_params("parallel", "arbitrary", "arbitrary"),
        name="kspec",
    )(cm, sm, filt, filt, filt)


def _dft_fwd_body(cm_ref, sm_ref, z_ref, kre_ref, kim_ref, yre_ref, yim_ref):
    z = z_ref[0].astype(BF16)
    zre = _dot(cm_ref[...], z)
    zim = _dot(sm_ref[...], z)
    kre = kre_ref[...]
    kim = kim_ref[...]
    yre_ref[0] = (zre * kre - zim * kim).astype(yre_ref.dtype)
    yim_ref[0] = (zre * kim + zim * kre).astype(yim_ref.dtype)

    @pl.when(pl.program_id(0) == 0)
    def _():
        r = 16
        row = lax.broadcasted_iota(jnp.int32, (r, zre.shape[1]), 0)
        a, b_, p, q = zre[0:r], zim[0:r], kre[0:r], kim[0:r]
        yre_ref[0, 0:r, :] = jnp.where(row == 0, a * p, a * p - b_ * q).astype(yre_ref.dtype)
        yim_ref[0, 0:r, :] = jnp.where(row == 0, b_ * q, a * q + b_ * p).astype(yim_ref.dtype)


def _dft_fwd(cm, sm, z, zoff, kre, kim, order, tf=1024, tn=512):
    b = z.shape[0]
    c = HYENA_WIDTH
    nc = c // tn
    nbins = cm.shape[0]
    out = jax.ShapeDtypeStruct((b, nbins, c), BF16)
    return pl.pallas_call(
        _dft_fwd_body,
        grid=(nbins // tf, nc, b),
        in_specs=[pl.BlockSpec((tf, SEQ), lambda f, j, i: (f, 0)),
                  pl.BlockSpec((tf, SEQ), lambda f, j, i: (f, 0)),
                  pl.BlockSpec((1, SEQ, tn), lambda f, j, i: (i, 0, zoff + j)),
                  pl.BlockSpec((tf, tn), lambda f, j, i: (f, order * nc + j)),
                  pl.BlockSpec((tf, tn), lambda f, j, i: (f, order * nc + j))],
        out_specs=[pl.BlockSpec((1, tf, tn), lambda f, j, i: (i, f, j))] * 2,
        out_shape=[out, out],
        compiler_params=_params("parallel", "arbitrary", "arbitrary"),
        name="dft_fwd",
    )(cm, sm, z, kre, kim)


def _dft_inv_body(ic_ref, im_ref, yre_ref, yim_ref, z_ref, g_ref, b_ref, gn_ref, o_ref, *, final):
    y = _dot(ic_ref[...], yre_ref[0]) + _dot(im_ref[...], yim_ref[0])
    zn = g_ref[0] * (y + b_ref[...] * z_ref[0])
    if not final:
        o_ref[0] = zn.astype(o_ref.dtype)
    else:
        for gi in range(zn.shape[1] // HYENA_GROUP):
            sl = slice(gi * HYENA_GROUP, (gi + 1) * HYENA_GROUP)
            blk = zn[:, sl]
            blk = blk * lax.rsqrt(jnp.mean(blk * blk, axis=-1, keepdims=True) + RMS_EPS)
            o_ref[0, :, sl] = (blk * gn_ref[:, sl]).astype(o_ref.dtype)


def _dft_inv(ic, im, yre, yim, z, zoff, gate, goff, bias, gnorm, final, tl=1024, tn=512):
    b = yre.shape[0]
    c = HYENA_WIDTH
    nbins = ic.shape[1]
    return pl.pallas_call(
        functools.partial(_dft_inv_body, final=final),
        grid=(SEQ // tl, c // tn, b),
        in_specs=[pl.BlockSpec((tl, nbins), lambda l, j, i: (l, 0)),
                  pl.BlockSpec((tl, nbins), lambda l, j, i: (l, 0)),
                  pl.BlockSpec((1, nbins, tn), lambda l, j, i: (i, 0, j)),
                  pl.BlockSpec((1, nbins, tn), lambda l, j, i: (i, 0, j)),
                  pl.BlockSpec((1, tl, tn), lambda l, j, i: (i, l, zoff + j)),
                  pl.BlockSpec((1, tl, tn), lambda l, j, i: (i, l, goff + j)),
                  pl.BlockSpec((1, tn), lambda l, j, i: (0, j)),
                  pl.BlockSpec((1, tn), lambda l, j, i: (0, j))],
        out_specs=pl.BlockSpec((1, tl, tn), lambda l, j, i: (i, l, j)),
        out_shape=jax.ShapeDtypeStruct((b, SEQ, c), BF16 if final else F32),
        compiler_params=_params("parallel", "arbitrary", "arbitrary"),
        name="dft_inv",
    )(ic, im, yre, yim, z, gate, bias.reshape(1, c), gnorm.reshape(1, c))


def _hyena(proj, dft, conv_w, conv_b, w1, b1, w2, b2, freq, w3, filt_bias, g_out, tn=512):
    cm, sm, ic, im = dft
    nc = HYENA_WIDTH // tn
    u = _shortconv(proj, conv_w, conv_b)
    filt = _hyena_filters(w1, b1, w2, b2, freq, w3)
    kre, kim = _kspec(cm, sm, filt)
    z, zoff = u, 0
    for order in range(HYENA_ORDER):
        final = order == HYENA_ORDER - 1
        yre, yim = _dft_fwd(cm, sm, z, zoff, kre, kim, order)
        z = _dft_inv(ic, im, yre, yim, z, zoff, u, (order + 1) * nc, filt_bias[order],
                     g_out, final)
        zoff = 0
    return z


def _rotary_tables():
    half = HEAD_DIM // 2
    pos = jnp.arange(SEQ, dtype=F32)
    inv = ROPE_THETA ** (-jnp.arange(half, dtype=F32) / half)
    ang = pos[:, None] * inv[None, :]
    cos = jnp.cos(ang)
    sin = jnp.sin(ang)
    return jnp.concatenate([cos, cos], axis=-1), jnp.concatenate([-sin, sin], axis=-1)


def _mixer(proj, dft, rot, q_norm, k_norm, conv_w, conv_b, w1, b1, w2, b2, freq, w3,
           filt_bias, attn_out_norm, hyena_out_norm):
    attn = _attention(proj, q_norm, k_norm, rot[0], rot[1], attn_out_norm)
    hy = _hyena(proj, dft, conv_w, conv_b, w1, b1, w2, b2, freq, w3, filt_bias, hyena_out_norm)
    return jnp.concatenate([attn, hy], axis=-1)


def _ffn(x, g, wg, wu, wd):
    pad = FFN_PAD - FFN_HIDDEN
    wg = jnp.pad(wg.astype(BF16), ((0, 0), (0, pad)))
    wu = jnp.pad(wu.astype(BF16), ((0, 0), (0, pad)))
    wd = jnp.pad(wd.astype(BF16), ((0, pad), (0, 0)))
    a = _ffn_up(_rmsnorm(x, g), wg, wu)
    return _mm_res(a, wd, x, 0.5, single_buffer_a=True)


def kernel(x, ffn_norm, ffn_w_gate, ffn_w_up, ffn_w_down, mix_norm, w_in, q_norm, k_norm, conv_w, conv_b, filt_w1, filt_b1, filt_w2, filt_b2, filt_freq, filt_w3, filt_bias, attn_out_norm, hyena_out_norm, w_out):
    b, s, d = x.shape
    x = x.reshape(b * s, d)
    dft = _dft_matrices()
    rot = _rotary_tables()
    for l in range(DEPTH):
        x = _ffn(x, ffn_norm[l, 0], ffn_w_gate[l, 0], ffn_w_up[l, 0], ffn_w_down[l, 0])
        proj = _mm(_rmsnorm(x, mix_norm[l]), w_in[l].astype(BF16), F32)
        merged = _mixer(proj.reshape(b, s, IN_COLS), dft, rot, q_norm[l], k_norm[l], conv_w[l],
                        conv_b[l], filt_w1[l], filt_b1[l], filt_w2[l], filt_b2[l], filt_freq[l],
                        filt_w3[l], filt_bias[l], attn_out_norm[l], hyena_out_norm[l])
        x = _mm_res(merged.reshape(b * s, d), w_out[l].astype(BF16), x, 1.0, tn=512)
        x = _ffn(x, ffn_norm[l, 1], ffn_w_gate[l, 1], ffn_w_up[l, 1], ffn_w_down[l, 1])
    return x.reshape(b, s, d)
```

```python
import functools
import math

import numpy as np
import jax
import jax.numpy as jnp
from jax import lax
from jax.experimental import pallas as pl
from jax.experimental.pallas import tpu as pltpu

D_MODEL = 4096
BATCH = 4
SEQ = 2048
DEPTH = 2
ATTN_WIDTH = D_MODEL // 2
HYENA_WIDTH = D_MODEL - ATTN_WIDTH
HEAD_DIM = 128
N_HEADS = ATTN_WIDTH // HEAD_DIM
DILATED_CONFIGS = ((128, 1), (512, 4), (2048, 16))
ROPE_THETA = 10000.0
HYENA_ORDER = 2
HYENA_GROUP = 128
FILTER_EMB_DIM = 33
FILTER_HIDDEN = 64
DECAY_FAST = 0.3
DECAY_SLOW = 1.5
DECAY_TARGET = 1e-2
FFN_HIDDEN = 256 * (-(-(8 * D_MODEL) // (3 * 256)))
IN_COLS = 3 * ATTN_WIDTH + (HYENA_ORDER + 1) * HYENA_WIDTH
RMS_EPS = 1e-6
NEG_INF = -1e30

F32 = jnp.float32
BF16 = jnp.bfloat16

V7X_VMEM_LIMIT_BYTES = 56 * 1024 * 1024
LANES = 128
ATTN_TQ = 256
ATTN_REACH = max((w // (2 * d)) * d for w, d in DILATED_CONFIGS)
DFT_N = 2 * SEQ


def _params(*sem):
    return pltpu.CompilerParams(dimension_semantics=sem, vmem_limit_bytes=V7X_VMEM_LIMIT_BYTES)


def _dot(a, b):
    return jnp.dot(a, b, preferred_element_type=F32)


def _rmsnorm_body(x_ref, g_ref, o_ref):
    x = x_ref[...]
    ms = jnp.mean(x * x, axis=-1, keepdims=True)
    o_ref[...] = (x * lax.rsqrt(ms + RMS_EPS) * g_ref[...]).astype(o_ref.dtype)


def _rmsnorm(x, g, tm=512):
    m, d = x.shape
    return pl.pallas_call(
        _rmsnorm_body,
        grid=(m // tm,),
        in_specs=[pl.BlockSpec((tm, d), lambda i: (i, 0)),
                  pl.BlockSpec((1, d), lambda i: (0, 0))],
        out_specs=pl.BlockSpec((tm, d), lambda i: (i, 0)),
        out_shape=jax.ShapeDtypeStruct((m, d), BF16),
        compiler_params=_params("parallel"),
        name="rmsnorm",
    )(x, g.reshape(1, d))


def _swiglu_body(a_ref, wg_ref, wu_ref, o_ref):
    a = a_ref[...]
    g = _dot(a, wg_ref[...].astype(BF16))
    u = _dot(a, wu_ref[...].astype(BF16))
    o_ref[...] = (g * jax.nn.sigmoid(g) * u).astype(o_ref.dtype)


def _mm_body(a_ref, w_ref, o_ref):
    o_ref[...] = _dot(a_ref[...], w_ref[...].astype(BF16)).astype(o_ref.dtype)


def _mm_res_body(a_ref, w_ref, r_ref, o_ref, *, scale):
    o_ref[...] = r_ref[...] + scale * _dot(a_ref[...], w_ref[...].astype(BF16))


def _matmul(body, a, weights, res, out_dtype, tm, tn, name):
    m, k = a.shape
    n = weights[0][0].shape[-1]
    in_specs = [pl.BlockSpec((tm, k), lambda i, j: (i, 0), pipeline_mode=pl.Buffered(1))]
    for w, prefix in weights:
        in_specs.append(pl.BlockSpec((None,) * len(prefix) + (k, tn),
                                     lambda i, j, prefix=prefix: prefix + (0, j)))
    operands = [a] + [w for w, _ in weights]
    if res is not None:
        in_specs.append(pl.BlockSpec((tm, tn), lambda i, j: (i, j)))
        operands.append(res)
    return pl.pallas_call(
        body,
        grid=(m // tm, n // tn),
        in_specs=in_specs,
        out_specs=pl.BlockSpec((tm, tn), lambda i, j: (i, j)),
        out_shape=jax.ShapeDtypeStruct((m, n), out_dtype),
        compiler_params=_params("parallel", "arbitrary"),
        name=name,
    )(*operands)


def _cast_body(w_ref, o_ref):
    o_ref[...] = w_ref[...].astype(o_ref.dtype)


def _cast_bf16(w, prefix, tr=256):
    k, n = w.shape[-2:]
    return pl.pallas_call(
        _cast_body,
        grid=(k // tr,),
        in_specs=[pl.BlockSpec((None,) * len(prefix) + (tr, n), lambda i: prefix + (i, 0))],
        out_specs=pl.BlockSpec((tr, n), lambda i: (i, 0)),
        out_shape=jax.ShapeDtypeStruct((k, n), BF16),
        compiler_params=_params("parallel"),
        name="cast_bf16",
    )(w)


def _attn_bias_table():
    il = np.arange(ATTN_TQ)[:, None]
    c = np.arange(ATTN_TQ + 2 * ATTN_REACH)[None, :]
    d = il + ATTN_REACH - c
    mult = np.zeros(d.shape, np.int64)
    for window, dil in DILATED_CONFIGS:
        half = window // (2 * dil)
        mult += ((d % dil) == 0) & (np.abs(d) <= half * dil)
    return np.where(mult > 0, np.log(np.maximum(mult, 1)), NEG_INF).astype(np.float32)


def _attn_body(q_ref, k_ref, v_ref, gq_ref, gk_ref, cos_ref, sin_ref, bias_ref, go_ref,
               o_ref, qs_ref, ks_ref, vs_ref):
    seq = q_ref.shape[1]
    cos = cos_ref[...]
    sin = sin_ref[...]

    def norm_rot(x, g):
        y = x * lax.rsqrt(jnp.mean(x * x, axis=-1, keepdims=True) + RMS_EPS) * g
        return y * cos + pltpu.roll(y, HEAD_DIM // 2, axis=1) * sin

    qs_ref[...] = (norm_rot(q_ref[0], gq_ref[...]) * (1.0 / math.sqrt(HEAD_DIM))).astype(BF16)
    ks_ref[...] = norm_rot(k_ref[0], gk_ref[...]).astype(BF16)
    vs_ref[...] = v_ref[0].astype(BF16)

    for t in range(seq // ATTN_TQ):
        q0 = t * ATTN_TQ
        lo = max(0, q0 - ATTN_REACH)
        hi = min(seq, q0 + ATTN_TQ + ATTN_REACH)
        c0 = lo - (q0 - ATTN_REACH)
        s = lax.dot_general(qs_ref[q0:q0 + ATTN_TQ, :], ks_ref[lo:hi, :],
                            (((1,), (1,)), ((), ())), preferred_element_type=F32)
        s = s + bias_ref[:, c0:c0 + (hi - lo)]
        m = jnp.max(s, axis=-1, keepdims=True)
        p = jnp.exp(s - m)
        den = jnp.sum(p, axis=-1, keepdims=True)
        o = _dot(p.astype(BF16), vs_ref[lo:hi, :]) / den
        o = o * lax.rsqrt(jnp.mean(o * o, axis=-1, keepdims=True) + RMS_EPS) * go_ref[...]
        o_ref[0, q0:q0 + ATTN_TQ, :] = o.astype(o_ref.dtype)


def _attention(proj, gq, gk, cos, sin_signed, g_out):
    b, s, _ = proj.shape
    hd = HEAD_DIM
    bias = jnp.asarray(_attn_bias_table())
    head = lambda off: pl.BlockSpec((1, s, hd), lambda i, h: (i, 0, off + h))
    const2 = lambda shape: pl.BlockSpec(shape, lambda i, h: (0, 0))
    return pl.pallas_call(
        _attn_body,
        grid=(b, N_HEADS),
        in_specs=[head(0), head(N_HEADS), head(2 * N_HEADS),
                  const2((1, hd)), const2((1, hd)),
                  const2((s, hd)), const2((s, hd)),
                  const2(bias.shape),
                  pl.BlockSpec((1, hd), lambda i, h: (0, h))],
        out_specs=pl.BlockSpec((1, s, hd), lambda i, h: (i, 0, h)),
        out_shape=jax.ShapeDtypeStruct((b, s, ATTN_WIDTH), BF16),
        scratch_shapes=[pltpu.VMEM((s, hd), BF16)] * 3,
        compiler_params=_params("parallel", "arbitrary"),
        name="attention",
    )(proj, proj, proj, gq.reshape(1, hd), gk.reshape(1, hd), cos, sin_signed, bias,
      g_out.reshape(1, ATTN_WIDTH))


def _shortconv_body(u_ref, w_ref, b_ref, o_ref):
    u = u_ref[0]
    seq = u.shape[0]
    row = lax.broadcasted_iota(jnp.int32, u.shape, 0)
    prev = jnp.where(row == 0, 0.0, pltpu.roll(u, 1, axis=0))
    nxt = jnp.where(row == seq - 1, 0.0, pltpu.roll(u, seq - 1, axis=0))
    o_ref[0] = w_ref[0:1, :] * prev + w_ref[1:2, :] * u + w_ref[2:3, :] * nxt + b_ref[...]


def _shortconv(proj, conv_w, conv_b, tc=512):
    b, s, _ = proj.shape
    width = conv_w.shape[1]
    off = (IN_COLS - width) // tc
    return pl.pallas_call(
        _shortconv_body,
        grid=(b, width // tc),
        in_specs=[pl.BlockSpec((1, s, tc), lambda i, j: (i, 0, off + j)),
                  pl.BlockSpec((3, tc), lambda i, j: (0, j)),
                  pl.BlockSpec((1, tc), lambda i, j: (0, j))],
        out_specs=pl.BlockSpec((1, s, tc), lambda i, j: (i, 0, j)),
        out_shape=jax.ShapeDtypeStruct((b, s, width), F32),
        compiler_params=_params("parallel", "arbitrary"),
        name="shortconv",
    )(proj, conv_w, conv_b.reshape(1, width))


def _filt_hidden_body(emb_ref, w1_ref, b1_ref, w2_ref, b2_ref, fr_ref, o_ref):
    hp = lax.Precision.HIGHEST
    h = jnp.dot(emb_ref[...], w1_ref[...], precision=hp, preferred_element_type=F32)
    h = jnp.sin(fr_ref[0:1, :] * (h + b1_ref[...]))
    h = jnp.dot(h, w2_ref[...], precision=hp, preferred_element_type=F32)
    o_ref[...] = jnp.sin(fr_ref[1:2, :] * (h + b2_ref[...]))


def _filt_out_body(h_ref, w3_ref, t_ref, delta_ref, o_ref):
    h = jnp.dot(h_ref[...], w3_ref[...], precision=lax.Precision.HIGHEST,
                preferred_element_type=F32)
    o_ref[...] = h * jnp.exp(-t_ref[...] * delta_ref[...])


def _pad_to(a, shape):
    return jnp.pad(a, [(0, t - s) for s, t in zip(a.shape, shape)])


def _hyena_filters(w1, b1, w2, b2, freq, w3, tn=512):
    seq = SEQ
    t = jnp.linspace(0.0, 1.0, seq, dtype=F32)[:, None]
    bands = (FILTER_EMB_DIM - 1) // 2
    f = jnp.linspace(1e-4, bands - 1, bands, dtype=F32)[None, :]
    wpos = (2.0 * math.pi) * jnp.arange(seq, dtype=F32)[:, None] / seq
    emb = jnp.concatenate([t, jnp.cos(f * wpos), -jnp.sin(f * wpos)], axis=-1)
    deltas = jnp.abs(jnp.linspace(math.log(DECAY_FAST) / DECAY_TARGET,
                                  math.log(DECAY_SLOW) / DECAY_TARGET, HYENA_WIDTH, dtype=F32))
    ncol = w3.shape[1]
    deltas = jnp.tile(deltas, ncol // HYENA_WIDTH).reshape(1, ncol)
    hid = LANES
    hidden = pl.pallas_call(
        _filt_hidden_body,
        out_shape=jax.ShapeDtypeStruct((seq, hid), F32),
        name="filt_hidden",
    )(_pad_to(emb, (seq, hid)), _pad_to(w1, (hid, hid)), _pad_to(b1[None], (1, hid)),
      _pad_to(w2, (hid, hid)), _pad_to(b2[None], (1, hid)), _pad_to(freq, (2, hid)))
    return pl.pallas_call(
        _filt_out_body,
        grid=(ncol // tn,),
        in_specs=[pl.BlockSpec((seq, hid), lambda j: (0, 0)),
                  pl.BlockSpec((hid, tn), lambda j: (0, j)),
                  pl.BlockSpec((seq, 1), lambda j: (0, 0)),
                  pl.BlockSpec((1, tn), lambda j: (0, j))],
        out_specs=pl.BlockSpec((seq, tn), lambda j: (0, j)),
        out_shape=jax.ShapeDtypeStruct((seq, ncol), F32),
        compiler_params=_params("parallel"),
        name="filt_out",
    )(hidden, _pad_to(w3, (hid, ncol)), t, deltas)


def _dft_matrices():
    n = jnp.arange(SEQ, dtype=jnp.int32)
    prod = (n[:, None] * n[None, :]) & (DFT_N - 1)
    ang = prod.astype(F32) * (2.0 * math.pi / DFT_N)
    c = jnp.cos(ang)
    s = jnp.sin(ang)
    alt = jnp.where(n % 2 == 0, 1.0, -1.0).astype(F32)
    first_row = (n == 0)[:, None]
    first_col = (n == 0)[None, :]
    cm = c
    sm = jnp.where(first_row, alt[None, :], -s)
    ic = c * jnp.where(first_col, 1.0 / DFT_N, 2.0 / DFT_N)
    im = jnp.where(first_col, alt[:, None] / DFT_N, -s * (2.0 / DFT_N))
    return cm.astype(BF16), sm.astype(BF16), ic.astype(BF16), im.astype(BF16)


def _kspec_body(cm_ref, sm_ref, hf_ref, hb_ref, hb0_ref, kre_ref, kim_ref):
    cm = cm_ref[...]
    sm = sm_ref[...]
    hf = hf_ref[...].astype(BF16)
    hb = hb_ref[...].astype(BF16)
    hb0 = hb0_ref[0:1, :]
    kre_ref[...] = _dot(cm, hf) + _dot(cm, hb) - hb0
    fim = _dot(sm, hf)
    bim = _dot(sm, hb)
    kim_ref[...] = fim - bim

    @pl.when(pl.program_id(0) == 0)
    def _():
        row = lax.broadcasted_iota(jnp.int32, (8, fim.shape[1]), 0)
        kim_ref[0:8, :] = jnp.where(row == 0, fim[0:8] + bim[0:8] - hb0, fim[0:8] - bim[0:8])


def _kspec(cm, sm, filt, tf=1024, tn=512):
    c = HYENA_WIDTH
    nc = c // tn
    nbins = cm.shape[0]
    out = jax.ShapeDtypeStruct((nbins, HYENA_ORDER * c), F32)
    return pl.pallas_call(
        _kspec_body,
        grid=(nbins // tf, HYENA_ORDER, nc),
        in_specs=[pl.BlockSpec((tf, SEQ), lambda f, o, j: (f, 0)),
                  pl.BlockSpec((tf, SEQ), lambda f, o, j: (f, 0)),
                  pl.BlockSpec((SEQ, tn), lambda f, o, j: (0, o * 2 * nc + j)),
                  pl.BlockSpec((SEQ, tn), lambda f, o, j: (0, o * 2 * nc + nc + j)),
                  pl.BlockSpec((8, tn), lambda f, o, j: (0, o * 2 * nc + nc + j))],
        out_specs=[pl.BlockSpec((tf, tn), lambda f, o, j: (f, o * nc + j))] * 2,
        out_shape=[out, out],
        compiler_params=_params("parallel", "arbitrary", "arbitrary"),
        name="kspec",
    )(cm, sm, filt, filt, filt)


def _dft_fwd_body(cm_ref, sm_ref, z_ref, kre_ref, kim_ref, yre_ref, yim_ref):
    z = z_ref[0].astype(BF16)
    zre = _dot(cm_ref[...], z)
    zim = _dot(sm_ref[...], z)
    kre = kre_ref[...]
    kim = kim_ref[...]
    yre_ref[0] = (zre * kre - zim * kim).astype(yre_ref.dtype)
    yim_ref[0] = (zre * kim + zim * kre).astype(yim_ref.dtype)

    @pl.when(pl.program_id(0) == 0)
    def _():
        r = 16
        row = lax.broadcasted_iota(jnp.int32, (r, zre.shape[1]), 0)
        a, b_, p, q = zre[0:r], zim[0:r], kre[0:r], kim[0:r]
        yre_ref[0, 0:r, :] = jnp.where(row == 0, a * p, a * p - b_ * q).astype(yre_ref.dtype)
        yim_ref[0, 0:r, :] = jnp.where(row == 0, b_ * q, a * q + b_ * p).astype(yim_ref.dtype)


def _dft_fwd(cm, sm, z, zoff, kre, kim, order, tf=1024, tn=512):
    b = z.shape[0]
    c = HYENA_WIDTH
    nc = c // tn
    nbins = cm.shape[0]
    out = jax.ShapeDtypeStruct((b, nbins, c), BF16)
    return pl.pallas_call(
        _dft_fwd_body,
        grid=(nbins // tf, nc, b),
        in_specs=[pl.BlockSpec((tf, SEQ), lambda f, j, i: (f, 0)),
                  pl.BlockSpec((tf, SEQ), lambda f, j, i: (f, 0)),
                  pl.BlockSpec((1, SEQ, tn), lambda f, j, i: (i, 0, zoff + j)),
                  pl.BlockSpec((tf, tn), lambda f, j, i: (f, order * nc + j)),
                  pl.BlockSpec((tf, tn), lambda f, j, i: (f, order * nc + j))],
        out_specs=[pl.BlockSpec((1, tf, tn), lambda f, j, i: (i, f, j))] * 2,
        out_shape=[out, out],
        compiler_params=_params("parallel", "arbitrary", "arbitrary"),
        name="dft_fwd",
    )(cm, sm, z, kre, kim)


def _dft_inv_body(ic_ref, im_ref, yre_ref, yim_ref, z_ref, g_ref, b_ref, gn_ref, o_ref, *, final):
    y = _dot(ic_ref[...], yre_ref[0]) + _dot(im_ref[...], yim_ref[0])
    zn = g_ref[0] * (y + b_ref[...] * z_ref[0])
    if not final:
        o_ref[0] = zn.astype(o_ref.dtype)
    else:
        for gi in range(zn.shape[1] // HYENA_GROUP):
            sl = slice(gi * HYENA_GROUP, (gi + 1) * HYENA_GROUP)
            blk = zn[:, sl]
            blk = blk * lax.rsqrt(jnp.mean(blk * blk, axis=-1, keepdims=True) + RMS_EPS)
            o_ref[0, :, sl] = (blk * gn_ref[:, sl]).astype(o_ref.dtype)


def _dft_inv(ic, im, yre, yim, z, zoff, gate, goff, bias, gnorm, final, tl=1024, tn=512):
    b = yre.shape[0]
    c = HYENA_WIDTH
    nbins = ic.shape[1]
    return pl.pallas_call(
        functools.partial(_dft_inv_body, final=final),
        grid=(SEQ // tl, c // tn, b),
        in_specs=[pl.BlockSpec((tl, nbins), lambda l, j, i: (l, 0)),
                  pl.BlockSpec((tl, nbins), lambda l, j, i: (l, 0)),
                  pl.BlockSpec((1, nbins, tn), lambda l, j, i: (i, 0, j)),
                  pl.BlockSpec((1, nbins, tn), lambda l, j, i: (i, 0, j)),
                  pl.BlockSpec((1, tl, tn), lambda l, j, i: (i, l, zoff + j)),
                  pl.BlockSpec((1, tl, tn), lambda l, j, i: (i, l, goff + j)),
                  pl.BlockSpec((1, tn), lambda l, j, i: (0, j)),
                  pl.BlockSpec((1, tn), lambda l, j, i: (0, j))],
        out_specs=pl.BlockSpec((1, tl, tn), lambda l, j, i: (i, l, j)),
        out_shape=jax.ShapeDtypeStruct((b, SEQ, c), BF16 if final else F32),
        compiler_params=_params("parallel", "arbitrary", "arbitrary"),
        name="dft_inv",
    )(ic, im, yre, yim, z, gate, bias.reshape(1, c), gnorm.reshape(1, c))


def _hyena(proj, dft, conv_w, conv_b, w1, b1, w2, b2, freq, w3, filt_bias, g_out, tn=512):
    cm, sm, ic, im = dft
    nc = HYENA_WIDTH // tn
    u = _shortconv(proj, conv_w, conv_b)
    filt = _hyena_filters(w1, b1, w2, b2, freq, w3)
    kre, kim = _kspec(cm, sm, filt)
    z, zoff = u, 0
    for order in range(HYENA_ORDER):
        final = order == HYENA_ORDER - 1
        yre, yim = _dft_fwd(cm, sm, z, zoff, kre, kim, order)
        z = _dft_inv(ic, im, yre, yim, z, zoff, u, (order + 1) * nc, filt_bias[order],
                     g_out, final)
        zoff = 0
    return z


def _rotary_tables():
    half = HEAD_DIM // 2
    pos = jnp.arange(SEQ, dtype=F32)
    inv = ROPE_THETA ** (-jnp.arange(half, dtype=F32) / half)
    ang = pos[:, None] * inv[None, :]
    cos = jnp.cos(ang)
    sin = jnp.sin(ang)
    return jnp.concatenate([cos, cos], axis=-1), jnp.concatenate([-sin, sin], axis=-1)


def _mixer(proj, dft, rot, q_norm, k_norm, conv_w, conv_b, w1, b1, w2, b2, freq, w3,
           filt_bias, attn_out_norm, hyena_out_norm):
    attn = _attention(proj, q_norm, k_norm, rot[0], rot[1], attn_out_norm)
    hy = _hyena(proj, dft, conv_w, conv_b, w1, b1, w2, b2, freq, w3, filt_bias, hyena_out_norm)
    return jnp.concatenate([attn, hy], axis=-1)


def _ffn(x, g, w_gate, w_up, w_down, idx):
    a = _matmul(_swiglu_body, _rmsnorm(x, g), [(w_gate, idx), (w_up, idx)], None, BF16,
                2048, 256, "ffn_up")
    wd = _cast_bf16(w_down, idx)
    return _matmul(functools.partial(_mm_res_body, scale=0.5), a, [(wd, ())], x, F32,
                   1024, 256, "ffn_down")


def kernel(x, ffn_norm, ffn_w_gate, ffn_w_up, ffn_w_down, mix_norm, w_in, q_norm, k_norm, conv_w, conv_b, filt_w1, filt_b1, filt_w2, filt_b2, filt_freq, filt_w3, filt_bias, attn_out_norm, hyena_out_norm, w_out):
    b, s, d = x.shape
    x = x.reshape(b * s, d)
    dft = _dft_matrices()
    rot = _rotary_tables()
    for l in range(DEPTH):
        x = _ffn(x, ffn_norm[l, 0], ffn_w_gate, ffn_w_up, ffn_w_down, (l, 0))
        proj = _matmul(_mm_body, _rmsnorm(x, mix_norm[l]), [(w_in, (l,))], None, F32,
                       2048, 256, "in_proj")
        merged = _mixer(proj.reshape(b, s, IN_COLS), dft, rot, q_norm[l], k_norm[l], conv_w[l],
                        conv_b[l], filt_w1[l], filt_b1[l], filt_w2[l], filt_b2[l], filt_freq[l],
                        filt_w3[l], filt_bias[l], attn_out_norm[l], hyena_out_norm[l])
        x = _matmul(functools.partial(_mm_res_body, scale=1.0), merged.reshape(b * s, d),
                    [(w_out, (l,))], x, F32, 2048, 256, "out_proj")
        x = _ffn(x, ffn_norm[l, 1], ffn_w_gate, ffn_w_up, ffn_w_down, (l, 1))
    return x.reshape(b, s, d)
```

```python
import functools
import math

import numpy as np
import jax
import jax.numpy as jnp
from jax import lax
from jax.experimental import pallas as pl
from jax.experimental.pallas import tpu as pltpu

D_MODEL = 4096
BATCH = 4
SEQ = 2048
DEPTH = 2
ATTN_WIDTH = D_MODEL // 2
HYENA_WIDTH = D_MODEL - ATTN_WIDTH
HEAD_DIM = 128
N_HEADS = ATTN_WIDTH // HEAD_DIM
DILATED_CONFIGS = ((128, 1), (512, 4), (2048, 16))
ROPE_THETA = 10000.0
HYENA_ORDER = 2
HYENA_GROUP = 128
FILTER_EMB_DIM = 33
FILTER_HIDDEN = 64
DECAY_FAST = 0.3
DECAY_SLOW = 1.5
DECAY_TARGET = 1e-2
FFN_HIDDEN = 256 * (-(-(8 * D_MODEL) // (3 * 256)))
IN_COLS = 3 * ATTN_WIDTH + (HYENA_ORDER + 1) * HYENA_WIDTH
RMS_EPS = 1e-6
NEG_INF = -1e30

F32 = jnp.float32
BF16 = jnp.bfloat16

V7X_VMEM_LIMIT_BYTES = 56 * 1024 * 1024
LANES = 128
ATTN_TQ = 256
ATTN_REACH = max((w // (2 * d)) * d for w, d in DILATED_CONFIGS)
DFT_N = 2 * SEQ
DFT_SPLIT = 32


def _params(*sem):
    return pltpu.CompilerParams(dimension_semantics=sem, vmem_limit_bytes=V7X_VMEM_LIMIT_BYTES)


def _dot(a, b):
    return jnp.dot(a, b, preferred_element_type=F32)


def _rmsnorm_body(x_ref, g_ref, o_ref):
    x = x_ref[...]
    ms = jnp.mean(x * x, axis=-1, keepdims=True)
    o_ref[...] = (x * lax.rsqrt(ms + RMS_EPS) * g_ref[...]).astype(o_ref.dtype)


def _rmsnorm(x, g, tm=512):
    m, d = x.shape
    return pl.pallas_call(
        _rmsnorm_body,
        grid=(m // tm,),
        in_specs=[pl.BlockSpec((tm, d), lambda i: (i, 0)),
                  pl.BlockSpec((1, d), lambda i: (0, 0))],
        out_specs=pl.BlockSpec((tm, d), lambda i: (i, 0)),
        out_shape=jax.ShapeDtypeStruct((m, d), BF16),
        compiler_params=_params("parallel"),
        name="rmsnorm",
    )(x, g.reshape(1, d))


def _swiglu_body(a_ref, wg_ref, wu_ref, o_ref):
    a = a_ref[...]
    g = _dot(a, wg_ref[...].astype(BF16))
    u = _dot(a, wu_ref[...].astype(BF16))
    o_ref[...] = (g * jax.nn.sigmoid(g) * u).astype(o_ref.dtype)


def _mm_body(a_ref, w_ref, o_ref):
    o_ref[...] = _dot(a_ref[...], w_ref[...].astype(BF16)).astype(o_ref.dtype)


def _mm_res_body(a_ref, w_ref, r_ref, o_ref, *, scale):
    o_ref[...] = r_ref[...] + scale * _dot(a_ref[...], w_ref[...].astype(BF16))


def _mm2_res_body(a1_ref, a2_ref, w_ref, r_ref, o_ref):
    k1 = a1_ref.shape[1]
    acc = _dot(a1_ref[...], w_ref[0:k1, :].astype(BF16))
    acc += _dot(a2_ref[...], w_ref[k1:, :].astype(BF16))
    o_ref[...] = r_ref[...] + acc


def _swiglu_cast_body(a_ref, wg_ref, wu_ref, wd_ref, o_ref, wdb_ref):
    _swiglu_body(a_ref, wg_ref, wu_ref, o_ref)

    @pl.when(pl.program_id(0) == 0)
    def _():
        wdb_ref[...] = wd_ref[...].astype(wdb_ref.dtype)


def _matmul(body, a_list, weights, res, out_dtype, tm, tn, name, cast_rows=None):
    m = a_list[0].shape[0]
    n = weights[0][0].shape[-1]
    k = sum(a.shape[1] for a in a_list)
    nj = n // tn
    in_specs = [pl.BlockSpec((tm, a.shape[1]), lambda i, j: (i, 0), pipeline_mode=pl.Buffered(1))
                for a in a_list]
    for w, prefix in weights:
        in_specs.append(pl.BlockSpec((None,) * len(prefix) + (k, tn),
                                     lambda i, j, prefix=prefix: prefix + (0, j)))
    operands = list(a_list) + [w for w, _ in weights]
    if res is not None:
        in_specs.append(pl.BlockSpec((tm, tn), lambda i, j: (i, j)))
        operands.append(res)
    out_specs = pl.BlockSpec((tm, tn), lambda i, j: (i, j))
    out_shape = jax.ShapeDtypeStruct((m, n), out_dtype)
    if cast_rows is not None:
        w, prefix = cast_rows
        cols = w.shape[-1]
        row_block = lambda i, j: jnp.where(i == 0, j, nj - 1)
        in_specs.append(pl.BlockSpec((None,) * len(prefix) + (tn, cols),
                                     lambda i, j: prefix + (row_block(i, j), 0)))
        operands.append(w)
        out_specs = [out_specs, pl.BlockSpec((tn, cols), lambda i, j: (row_block(i, j), 0))]
        out_shape = [out_shape, jax.ShapeDtypeStruct((n, cols), BF16)]
    return pl.pallas_call(
        body,
        grid=(m // tm, nj),
        in_specs=in_specs,
        out_specs=out_specs,
        out_shape=out_shape,
        compiler_params=_params("arbitrary", "arbitrary"),
        name=name,
    )(*operands)


def _attn_bias_table():
    il = np.arange(ATTN_TQ)[:, None]
    c = np.arange(ATTN_TQ + 2 * ATTN_REACH)[None, :]
    d = il + ATTN_REACH - c
    mult = np.zeros(d.shape, np.int64)
    for window, dil in DILATED_CONFIGS:
        half = window // (2 * dil)
        mult += ((d % dil) == 0) & (np.abs(d) <= half * dil)
    return np.where(mult > 0, np.log(np.maximum(mult, 1)), NEG_INF).astype(np.float32)


def _attn_body(q_ref, k_ref, v_ref, gq_ref, gk_ref, cos_ref, sin_ref, bias_ref, go_ref,
               o_ref, qs_ref, ks_ref, vs_ref):
    seq = q_ref.shape[1]
    cos = cos_ref[...]
    sin = sin_ref[...]

    def norm_rot(x, g):
        y = x * lax.rsqrt(jnp.mean(x * x, axis=-1, keepdims=True) + RMS_EPS) * g
        return y * cos + pltpu.roll(y, HEAD_DIM // 2, axis=1) * sin

    qs_ref[...] = (norm_rot(q_ref[0], gq_ref[...]) * (1.0 / math.sqrt(HEAD_DIM))).astype(BF16)
    ks_ref[...] = norm_rot(k_ref[0], gk_ref[...]).astype(BF16)
    vs_ref[...] = v_ref[0].astype(BF16)

    for t in range(seq // ATTN_TQ):
        q0 = t * ATTN_TQ
        lo = max(0, q0 - ATTN_REACH)
        hi = min(seq, q0 + ATTN_TQ + ATTN_REACH)
        c0 = lo - (q0 - ATTN_REACH)
        s = lax.dot_general(qs_ref[q0:q0 + ATTN_TQ, :], ks_ref[lo:hi, :],
                            (((1,), (1,)), ((), ())), preferred_element_type=F32)
        s = s + bias_ref[:, c0:c0 + (hi - lo)]
        m = jnp.max(s, axis=-1, keepdims=True)
        p = jnp.exp(s - m)
        den = jnp.sum(p, axis=-1, keepdims=True)
        o = _dot(p.astype(BF16), vs_ref[lo:hi, :]) / den
        o = o * lax.rsqrt(jnp.mean(o * o, axis=-1, keepdims=True) + RMS_EPS) * go_ref[...]
        o_ref[0, q0:q0 + ATTN_TQ, :] = o.astype(o_ref.dtype)


def _attention(proj, gq, gk, cos, sin_signed, g_out):
    b, s, _ = proj.shape
    hd = HEAD_DIM
    bias = jnp.asarray(_attn_bias_table())
    head = lambda off: pl.BlockSpec((1, s, hd), lambda i, h: (i, 0, off + h))
    const2 = lambda shape: pl.BlockSpec(shape, lambda i, h: (0, 0))
    return pl.pallas_call(
        _attn_body,
        grid=(b, N_HEADS),
        in_specs=[head(0), head(N_HEADS), head(2 * N_HEADS),
                  const2((1, hd)), const2((1, hd)),
                  const2((s, hd)), const2((s, hd)),
                  const2(bias.shape),
                  pl.BlockSpec((1, hd), lambda i, h: (0, h))],
        out_specs=pl.BlockSpec((1, s, hd), lambda i, h: (i, 0, h)),
        out_shape=jax.ShapeDtypeStruct((b, s, ATTN_WIDTH), BF16),
        scratch_shapes=[pltpu.VMEM((s, hd), BF16)] * 3,
        compiler_params=_params("parallel", "arbitrary"),
        name="attention",
    )(proj, proj, proj, gq.reshape(1, hd), gk.reshape(1, hd), cos, sin_signed, bias,
      g_out.reshape(1, ATTN_WIDTH))


def _shortconv(u, w_ref, b_ref):
    seq, edge = u.shape[0], 8
    w0, w1, w2, b = w_ref[0:1, :], w_ref[1:2, :], w_ref[2:3, :], b_ref[...]
    taps = lambda prev, cur, nxt: w0 * prev + w1 * cur + w2 * nxt + b
    prev = pltpu.roll(u, 1, axis=0)
    nxt = pltpu.roll(u, seq - 1, axis=0)
    row = lax.broadcasted_iota(jnp.int32, (edge, u.shape[1]), 0)
    top = taps(jnp.where(row == 0, 0.0, prev[:edge]), u[:edge], nxt[:edge])
    bot = taps(prev[seq - edge:], u[seq - edge:], jnp.where(row == edge - 1, 0.0, nxt[seq - edge:]))
    return jnp.concatenate([top, taps(prev, u, nxt)[edge:seq - edge], bot], axis=0)


def _filt_hidden_body(emb_ref, w1_ref, b1_ref, w2_ref, b2_ref, fr_ref, o_ref):
    hp = lax.Precision.HIGHEST
    h = jnp.dot(emb_ref[...], w1_ref[...], precision=hp, preferred_element_type=F32)
    h = jnp.sin(fr_ref[0:1, :] * (h + b1_ref[...]))
    h = jnp.dot(h, w2_ref[...], precision=hp, preferred_element_type=F32)
    o_ref[...] = jnp.sin(fr_ref[1:2, :] * (h + b2_ref[...]))


def _filt_out_body(h_ref, w3_ref, t_ref, delta_ref, o_ref):
    h = jnp.dot(h_ref[...], w3_ref[...], precision=lax.Precision.HIGHEST,
                preferred_element_type=F32)
    o_ref[...] = h * jnp.exp(-t_ref[...] * delta_ref[...])


def _pad_to(a, shape):
    return jnp.pad(a, [(0, t - s) for s, t in zip(a.shape, shape)])


def _hyena_filters(w1, b1, w2, b2, freq, w3, tn=512):
    seq = SEQ
    t = jnp.linspace(0.0, 1.0, seq, dtype=F32)[:, None]
    bands = (FILTER_EMB_DIM - 1) // 2
    f = jnp.linspace(1e-4, bands - 1, bands, dtype=F32)[None, :]
    wpos = (2.0 * math.pi) * jnp.arange(seq, dtype=F32)[:, None] / seq
    emb = jnp.concatenate([t, jnp.cos(f * wpos), -jnp.sin(f * wpos)], axis=-1)
    deltas = jnp.abs(jnp.linspace(math.log(DECAY_FAST) / DECAY_TARGET,
                                  math.log(DECAY_SLOW) / DECAY_TARGET, HYENA_WIDTH, dtype=F32))
    ncol = w3.shape[1]
    deltas = jnp.tile(deltas, ncol // HYENA_WIDTH).reshape(1, ncol)
    hid = LANES
    hidden = pl.pallas_call(
        _filt_hidden_body,
        out_shape=jax.ShapeDtypeStruct((seq, hid), F32),
        name="filt_hidden",
    )(_pad_to(emb, (seq, hid)), _pad_to(w1, (hid, hid)), _pad_to(b1[None], (1, hid)),
      _pad_to(w2, (hid, hid)), _pad_to(b2[None], (1, hid)), _pad_to(freq, (2, hid)))
    return pl.pallas_call(
        _filt_out_body,
        grid=(ncol // tn,),
        in_specs=[pl.BlockSpec((seq, hid), lambda j: (0, 0)),
                  pl.BlockSpec((hid, tn), lambda j: (0, j)),
                  pl.BlockSpec((seq, 1), lambda j: (0, 0)),
                  pl.BlockSpec((1, tn), lambda j: (0, j))],
        out_specs=pl.BlockSpec((seq, tn), lambda j: (0, j)),
        out_shape=jax.ShapeDtypeStruct((seq, ncol), F32),
        compiler_params=_params("parallel"),
        name="filt_out",
    )(hidden, _pad_to(w3, (hid, ncol)), t, deltas)


def _dft_matrices():
    n = jnp.arange(SEQ, dtype=jnp.int32)
    def table(f):
        ang = ((f[:, None] * n[None, :]) & (DFT_N - 1)).astype(F32) * (2.0 * math.pi / DFT_N)
        return jnp.cos(ang), jnp.sin(ang)
    c_hi, s_hi = (t[:, None, :] for t in table(jnp.arange(0, SEQ, DFT_SPLIT, dtype=jnp.int32)))
    c_lo, s_lo = (t[None, :, :] for t in table(jnp.arange(DFT_SPLIT, dtype=jnp.int32)))
    c = (c_hi * c_lo - s_hi * s_lo).reshape(SEQ, SEQ)
    s = (s_hi * c_lo + c_hi * s_lo).reshape(SEQ, SEQ)
    alt = jnp.where(n % 2 == 0, 1.0, -1.0).astype(F32)
    first_row = (n == 0)[:, None]
    first_col = (n == 0)[None, :]
    cm = c
    sm = jnp.where(first_row, alt[None, :], -s)
    ic = c * jnp.where(first_col, 1.0 / DFT_N, 2.0 / DFT_N)
    im = jnp.where(first_col, alt[:, None] / DFT_N, -s * (2.0 / DFT_N))
    return cm.astype(BF16), sm.astype(BF16), ic.astype(BF16), im.astype(BF16)


def _kspec_body(cm_ref, sm_ref, hf_ref, hb_ref, hb0_ref, kre_ref, kim_ref):
    cm = cm_ref[...]
    sm = sm_ref[...]
    hf = hf_ref[...].astype(BF16)
    hb = hb_ref[...].astype(BF16)
    hb0 = hb0_ref[0:1, :]
    kre_ref[...] = _dot(cm, hf) + _dot(cm, hb) - hb0
    fim = _dot(sm, hf)
    bim = _dot(sm, hb)
    kim_ref[...] = fim - bim

    @pl.when(pl.program_id(0) == 0)
    def _():
        row = lax.broadcasted_iota(jnp.int32, (8, fim.shape[1]), 0)
        kim_ref[0:8, :] = jnp.where(row == 0, fim[0:8] + bim[0:8] - hb0, fim[0:8] - bim[0:8])


def _kspec(cm, sm, filt, tf=1024, tn=512):
    c = HYENA_WIDTH
    nc = c // tn
    nbins = cm.shape[0]
    out = jax.ShapeDtypeStruct((nbins, HYENA_ORDER * c), F32)
    return pl.pallas_call(
        _kspec_body,
        grid=(nbins // tf, HYENA_ORDER, nc),
        in_specs=[pl.BlockSpec((tf, SEQ), lambda f, o, j: (f, 0)),
                  pl.BlockSpec((tf, SEQ), lambda f, o, j: (f, 0)),
                  pl.BlockSpec((SEQ, tn), lambda f, o, j: (0, o * 2 * nc + j)),
                  pl.BlockSpec((SEQ, tn), lambda f, o, j: (0, o * 2 * nc + nc + j)),
                  pl.BlockSpec((8, tn), lambda f, o, j: (0, o * 2 * nc + nc + j))],
        out_specs=[pl.BlockSpec((tf, tn), lambda f, o, j: (f, o * nc + j))] * 2,
        out_shape=[out, out],
        compiler_params=_params("parallel", "arbitrary", "arbitrary"),
        name="kspec",
    )(cm, sm, filt, filt, filt)


def _dft_fwd_body(*refs, conv):
    if conv:
        cm_ref, sm_ref, z_ref, cw_ref, cb_ref, kre_ref, kim_ref, yre_ref, yim_ref = refs
        z = _shortconv(z_ref[0], cw_ref, cb_ref)
    else:
        cm_ref, sm_ref, z_ref, kre_ref, kim_ref, yre_ref, yim_ref = refs
        z = z_ref[0]
    z = z.astype(BF16)
    zre = _dot(cm_ref[...], z)
    zim = _dot(sm_ref[...], z)
    kre = kre_ref[...]
    kim = kim_ref[...]
    yre_ref[0] = (zre * kre - zim * kim).astype(yre_ref.dtype)
    yim_ref[0] = (zre * kim + zim * kre).astype(yim_ref.dtype)

    @pl.when(pl.program_id(0) == 0)
    def _():
        r = 16
        row = lax.broadcasted_iota(jnp.int32, (r, zre.shape[1]), 0)
        a, b_, p, q = zre[0:r], zim[0:r], kre[0:r], kim[0:r]
        yre_ref[0, 0:r, :] = jnp.where(row == 0, a * p, a * p - b_ * q).astype(yre_ref.dtype)
        yim_ref[0, 0:r, :] = jnp.where(row == 0, b_ * q, a * q + b_ * p).astype(yim_ref.dtype)


def _dft_fwd(cm, sm, z, zcol, conv, kre, kim, order, tf=1024, tn=512):
    b = z.shape[0]
    c = HYENA_WIDTH
    nc = c // tn
    nbins = cm.shape[0]
    out = jax.ShapeDtypeStruct((b, nbins, c), BF16)
    in_specs = [pl.BlockSpec((tf, SEQ), lambda f, j, i: (f, 0)),
                pl.BlockSpec((tf, SEQ), lambda f, j, i: (f, 0)),
                pl.BlockSpec((1, SEQ, tn), lambda f, j, i: (i, 0, zcol // tn + j))]
    operands = [cm, sm, z]
    if conv is not None:
        cw, cb, col = conv
        in_specs += [pl.BlockSpec((cw.shape[0], tn), lambda f, j, i: (0, col // tn + j)),
                     pl.BlockSpec((1, tn), lambda f, j, i: (0, col // tn + j))]
        operands += [cw, cb]
    in_specs += [pl.BlockSpec((tf, tn), lambda f, j, i: (f, order * nc + j))] * 2
    operands += [kre, kim]
    return pl.pallas_call(
        functools.partial(_dft_fwd_body, conv=conv is not None),
        grid=(nbins // tf, nc, b),
        in_specs=in_specs,
        out_specs=[pl.BlockSpec((1, tf, tn), lambda f, j, i: (i, f, j))] * 2,
        out_shape=[out, out],
        compiler_params=_params("parallel", "arbitrary", "arbitrary"),
        name="dft_fwd",
    )(*operands)


def _dft_inv_body(*refs, conv_z, final):
    if conv_z:
        (ic_ref, im_ref, yre_ref, yim_ref, z_ref, zw_ref, zb_ref, g_ref, gw_ref, gb_ref,
         b_ref, gn_ref, o_ref) = refs
        z = _shortconv(z_ref[0], zw_ref, zb_ref)
    else:
        ic_ref, im_ref, yre_ref, yim_ref, z_ref, g_ref, gw_ref, gb_ref, b_ref, gn_ref, o_ref = refs
        z = z_ref[0]
    gate = _shortconv(g_ref[0], gw_ref, gb_ref)
    y = _dot(ic_ref[...], yre_ref[0]) + _dot(im_ref[...], yim_ref[0])
    zn = gate * (y + b_ref[...] * z)
    if not final:
        o_ref[0] = zn.astype(o_ref.dtype)
    else:
        for gi in range(zn.shape[1] // HYENA_GROUP):
            sl = slice(gi * HYENA_GROUP, (gi + 1) * HYENA_GROUP)
            blk = zn[:, sl]
            blk = blk * lax.rsqrt(jnp.mean(blk * blk, axis=-1, keepdims=True) + RMS_EPS)
            o_ref[0, :, sl] = (blk * gn_ref[:, sl]).astype(o_ref.dtype)


def _dft_inv(ic, im, yre, yim, z, zcol, conv_z, gate, gcol, conv_g, bias, gnorm, final, tn=256):
    b = yre.shape[0]
    c = HYENA_WIDTH
    nbins = ic.shape[1]
    const = lambda shape: pl.BlockSpec(shape, lambda j, i: (0, 0), pipeline_mode=pl.Buffered(1))
    taps = lambda cw, col: [pl.BlockSpec((cw.shape[0], tn), lambda j, i: (0, col // tn + j)),
                            pl.BlockSpec((1, tn), lambda j, i: (0, col // tn + j))]
    in_specs = [const((SEQ, nbins)), const((SEQ, nbins)),
                pl.BlockSpec((1, nbins, tn), lambda j, i: (i, 0, j)),
                pl.BlockSpec((1, nbins, tn), lambda j, i: (i, 0, j)),
                pl.BlockSpec((1, SEQ, tn), lambda j, i: (i, 0, zcol // tn + j))]
    operands = [ic, im, yre, yim, z]
    if conv_z is not None:
        in_specs += taps(conv_z[0], conv_z[2])
        operands += list(conv_z[:2])
    in_specs += [pl.BlockSpec((1, SEQ, tn), lambda j, i: (i, 0, gcol // tn + j))]
    in_specs += taps(conv_g[0], conv_g[2])
    in_specs += [pl.BlockSpec((1, tn), lambda j, i: (0, j))] * 2
    operands += [gate, conv_g[0], conv_g[1], bias.reshape(1, c), gnorm.reshape(1, c)]
    return pl.pallas_call(
        functools.partial(_dft_inv_body, conv_z=conv_z is not None, final=final),
        grid=(c // tn, b),
        in_specs=in_specs,
        out_specs=pl.BlockSpec((1, SEQ, tn), lambda j, i: (i, 0, j)),
        out_shape=jax.ShapeDtypeStruct((b, SEQ, c), BF16 if final else F32),
        compiler_params=_params("parallel", "arbitrary"),
        name="dft_inv",
    )(*operands)


def _hyena(proj, dft, conv_w, conv_b, w1, b1, w2, b2, freq, w3, filt_bias, g_out):
    cm, sm, ic, im = dft
    c = HYENA_WIDTH
    raw = IN_COLS - (HYENA_ORDER + 1) * c
    conv_b = conv_b.reshape(1, -1)
    filt = _hyena_filters(w1, b1, w2, b2, freq, w3)
    kre, kim = _kspec(cm, sm, filt)
    z, zcol, conv_z = proj, raw, (conv_w, conv_b, 0)
    for order in range(HYENA_ORDER):
        final = order == HYENA_ORDER - 1
        yre, yim = _dft_fwd(cm, sm, z, zcol, conv_z, kre, kim, order)
        z = _dft_inv(ic, im, yre, yim, z, zcol, conv_z, proj, raw + (order + 1) * c,
                     (conv_w, conv_b, (order + 1) * c), filt_bias[order], g_out, final)
        zcol, conv_z = 0, None
    return z


def _rotary_tables():
    half = HEAD_DIM // 2
    pos = jnp.arange(SEQ, dtype=F32)
    inv = ROPE_THETA ** (-jnp.arange(half, dtype=F32) / half)
    ang = pos[:, None] * inv[None, :]
    cos = jnp.cos(ang)
    sin = jnp.sin(ang)
    return jnp.concatenate([cos, cos], axis=-1), jnp.concatenate([-sin, sin], axis=-1)


def _mixer(proj, dft, rot, q_norm, k_norm, conv_w, conv_b, w1, b1, w2, b2, freq, w3,
           filt_bias, attn_out_norm, hyena_out_norm):
    attn = _attention(proj, q_norm, k_norm, rot[0], rot[1], attn_out_norm)
    hy = _hyena(proj, dft, conv_w, conv_b, w1, b1, w2, b2, freq, w3, filt_bias, hyena_out_norm)
    return attn, hy


def _ffn(x, g, w_gate, w_up, w_down, idx):
    a, wd = _matmul(_swiglu_cast_body, [_rmsnorm(x, g)], [(w_gate, idx), (w_up, idx)], None, BF16,
                    2048, 256, "ffn_up", cast_rows=(w_down, idx))
    return _matmul(functools.partial(_mm_res_body, scale=0.5), [a], [(wd, ())], x, F32,
                   1024, 256, "ffn_down")


def kernel(x, ffn_norm, ffn_w_gate, ffn_w_up, ffn_w_down, mix_norm, w_in, q_norm, k_norm, conv_w, conv_b, filt_w1, filt_b1, filt_w2, filt_b2, filt_freq, filt_w3, filt_bias, attn_out_norm, hyena_out_norm, w_out):
    b, s, d = x.shape
    x = x.reshape(b * s, d)
    dft = _dft_matrices()
    rot = _rotary_tables()
    for l in range(DEPTH):
        x = _ffn(x, ffn_norm[l, 0], ffn_w_gate, ffn_w_up, ffn_w_down, (l, 0))
        proj = _matmul(_mm_body, [_rmsnorm(x, mix_norm[l])], [(w_in, (l,))], None, F32,
                       2048, 256, "in_proj")
        attn, hy = _mixer(proj.reshape(b, s, IN_COLS), dft, rot, q_norm[l], k_norm[l], conv_w[l],
                          conv_b[l], filt_w1[l], filt_b1[l], filt_w2[l], filt_b2[l], filt_freq[l],
                          filt_w3[l], filt_bias[l], attn_out_norm[l], hyena_out_norm[l])
        x = _matmul(_mm2_res_body, [attn.reshape(b * s, -1), hy.reshape(b * s, -1)],
                    [(w_out, (l,))], x, F32, 2048, 256, "out_proj")
        x = _ffn(x, ffn_norm[l, 1], ffn_w_gate, ffn_w_up, ffn_w_down, (l, 1))
    return x.reshape(b, s, d)
```

```python
import functools
import math

import numpy as np
import jax
import jax.numpy as jnp
from jax import lax
from jax.experimental import pallas as pl
from jax.experimental.pallas import tpu as pltpu

D_MODEL = 4096
BATCH = 4
SEQ = 2048
DEPTH = 2
ATTN_WIDTH = D_MODEL // 2
HYENA_WIDTH = D_MODEL - ATTN_WIDTH
HEAD_DIM = 128
N_HEADS = ATTN_WIDTH // HEAD_DIM
DILATED_CONFIGS = ((128, 1), (512, 4), (2048, 16))
ROPE_THETA = 10000.0
HYENA_ORDER = 2
HYENA_GROUP = 128
FILTER_EMB_DIM = 33
FILTER_HIDDEN = 64
DECAY_FAST = 0.3
DECAY_SLOW = 1.5
DECAY_TARGET = 1e-2
FFN_HIDDEN = 256 * (-(-(8 * D_MODEL) // (3 * 256)))
IN_COLS = 3 * ATTN_WIDTH + (HYENA_ORDER + 1) * HYENA_WIDTH
RMS_EPS = 1e-6
NEG_INF = -1e30

F32 = jnp.float32
BF16 = jnp.bfloat16

V7X_VMEM_LIMIT_BYTES = 56 * 1024 * 1024
LANES = 128
ATTN_TQ = 256
ATTN_REACH = max((w // (2 * d)) * d for w, d in DILATED_CONFIGS)
DFT_N = 2 * SEQ
DFT_HALF = SEQ // 2
DFT_SPLIT = 32


def _params(*sem):
    return pltpu.CompilerParams(dimension_semantics=sem, vmem_limit_bytes=V7X_VMEM_LIMIT_BYTES)


def _dot(a, b):
    return jnp.dot(a, b, preferred_element_type=F32)


def _rmsnorm_body(x_ref, g_ref, o_ref):
    x = x_ref[...]
    ms = jnp.mean(x * x, axis=-1, keepdims=True)
    o_ref[...] = (x * lax.rsqrt(ms + RMS_EPS) * g_ref[...]).astype(o_ref.dtype)


def _rmsnorm(x, g, tm=512):
    m, d = x.shape
    return pl.pallas_call(
        _rmsnorm_body,
        grid=(m // tm,),
        in_specs=[pl.BlockSpec((tm, d), lambda i: (i, 0)),
                  pl.BlockSpec((1, d), lambda i: (0, 0))],
        out_specs=pl.BlockSpec((tm, d), lambda i: (i, 0)),
        out_shape=jax.ShapeDtypeStruct((m, d), BF16),
        compiler_params=_params("parallel"),
        name="rmsnorm",
    )(x, g.reshape(1, d))


def _swiglu_body(a_ref, wg_ref, wu_ref, o_ref):
    a = a_ref[...]
    g = _dot(a, wg_ref[...].astype(BF16))
    u = _dot(a, wu_ref[...].astype(BF16))
    o_ref[...] = (g * jax.nn.sigmoid(g) * u).astype(o_ref.dtype)


def _mm_body(a_ref, w_ref, o_ref):
    o_ref[...] = _dot(a_ref[...], w_ref[...].astype(BF16)).astype(o_ref.dtype)


def _mm_res_body(a_ref, w_ref, r_ref, o_ref, *, scale):
    o_ref[...] = r_ref[...] + scale * _dot(a_ref[...], w_ref[...].astype(BF16))


def _mm2_res_body(a1_ref, a2_ref, w_ref, r_ref, o_ref):
    k1 = a1_ref.shape[1]
    acc = _dot(a1_ref[...], w_ref[0:k1, :].astype(BF16))
    acc += _dot(a2_ref[...], w_ref[k1:, :].astype(BF16))
    o_ref[...] = r_ref[...] + acc


def _swiglu_cast_body(a_ref, wg_ref, wu_ref, wd_ref, o_ref, wdb_ref):
    _swiglu_body(a_ref, wg_ref, wu_ref, o_ref)

    @pl.when(pl.program_id(0) == 0)
    def _():
        wdb_ref[...] = wd_ref[...].astype(wdb_ref.dtype)


def _matmul(body, a_list, weights, res, out_dtype, tm, tn, name, cast_rows=None):
    m = a_list[0].shape[0]
    n = weights[0][0].shape[-1]
    k = sum(a.shape[1] for a in a_list)
    nj = n // tn
    in_specs = [pl.BlockSpec((tm, a.shape[1]), lambda i, j: (i, 0), pipeline_mode=pl.Buffered(1))
                for a in a_list]
    for w, prefix in weights:
        in_specs.append(pl.BlockSpec((None,) * len(prefix) + (k, tn),
                                     lambda i, j, prefix=prefix: prefix + (0, j)))
    operands = list(a_list) + [w for w, _ in weights]
    if res is not None:
        in_specs.append(pl.BlockSpec((tm, tn), lambda i, j: (i, j)))
        operands.append(res)
    out_specs = pl.BlockSpec((tm, tn), lambda i, j: (i, j))
    out_shape = jax.ShapeDtypeStruct((m, n), out_dtype)
    if cast_rows is not None:
        w, prefix = cast_rows
        cols = w.shape[-1]
        row_block = lambda i, j: jnp.where(i == 0, j, nj - 1)
        in_specs.append(pl.BlockSpec((None,) * len(prefix) + (tn, cols),
                                     lambda i, j: prefix + (row_block(i, j), 0)))
        operands.append(w)
        out_specs = [out_specs, pl.BlockSpec((tn, cols), lambda i, j: (row_block(i, j), 0))]
        out_shape = [out_shape, jax.ShapeDtypeStruct((n, cols), BF16)]
    return pl.pallas_call(
        body,
        grid=(m // tm, nj),
        in_specs=in_specs,
        out_specs=out_specs,
        out_shape=out_shape,
        compiler_params=_params("arbitrary", "arbitrary"),
        name=name,
    )(*operands)


def _attn_bias_table():
    il = np.arange(ATTN_TQ)[:, None]
    c = np.arange(ATTN_TQ + 2 * ATTN_REACH)[None, :]
    d = il + ATTN_REACH - c
    mult = np.zeros(d.shape, np.int64)
    for window, dil in DILATED_CONFIGS:
        half = window // (2 * dil)
        mult += ((d % dil) == 0) & (np.abs(d) <= half * dil)
    return np.where(mult > 0, np.log(np.maximum(mult, 1)), NEG_INF).astype(np.float32)


def _attn_body(q_ref, k_ref, v_ref, gq_ref, gk_ref, cos_ref, sin_ref, bias_ref, go_ref,
               o_ref, qs_ref, ks_ref, vs_ref):
    seq = q_ref.shape[1]
    cos = cos_ref[...]
    sin = sin_ref[...]

    def norm_rot(x, g):
        y = x * lax.rsqrt(jnp.mean(x * x, axis=-1, keepdims=True) + RMS_EPS) * g
        return y * cos + pltpu.roll(y, HEAD_DIM // 2, axis=1) * sin

    qs_ref[...] = (norm_rot(q_ref[0], gq_ref[...]) * (1.0 / math.sqrt(HEAD_DIM))).astype(BF16)
    ks_ref[...] = norm_rot(k_ref[0], gk_ref[...]).astype(BF16)
    vs_ref[...] = v_ref[0].astype(BF16)

    for t in range(seq // ATTN_TQ):
        q0 = t * ATTN_TQ
        lo = max(0, q0 - ATTN_REACH)
        hi = min(seq, q0 + ATTN_TQ + ATTN_REACH)
        c0 = lo - (q0 - ATTN_REACH)
        s = lax.dot_general(qs_ref[q0:q0 + ATTN_TQ, :], ks_ref[lo:hi, :],
                            (((1,), (1,)), ((), ())), preferred_element_type=F32)
        s = s + bias_ref[:, c0:c0 + (hi - lo)]
        m = jnp.max(s, axis=-1, keepdims=True)
        p = jnp.exp(s - m)
        den = jnp.sum(p, axis=-1, keepdims=True)
        o = _dot(p.astype(BF16), vs_ref[lo:hi, :]) / den
        o = o * lax.rsqrt(jnp.mean(o * o, axis=-1, keepdims=True) + RMS_EPS) * go_ref[...]
        o_ref[0, q0:q0 + ATTN_TQ, :] = o.astype(o_ref.dtype)


def _attention(proj, gq, gk, cos, sin_signed, g_out):
    b, s, _ = proj.shape
    hd = HEAD_DIM
    bias = jnp.asarray(_attn_bias_table())
    head = lambda off: pl.BlockSpec((1, s, hd), lambda i, h: (i, 0, off + h))
    const2 = lambda shape: pl.BlockSpec(shape, lambda i, h: (0, 0))
    return pl.pallas_call(
        _attn_body,
        grid=(b, N_HEADS),
        in_specs=[head(0), head(N_HEADS), head(2 * N_HEADS),
                  const2((1, hd)), const2((1, hd)),
                  const2((s, hd)), const2((s, hd)),
                  const2(bias.shape),
                  pl.BlockSpec((1, hd), lambda i, h: (0, h))],
        out_specs=pl.BlockSpec((1, s, hd), lambda i, h: (i, 0, h)),
        out_shape=jax.ShapeDtypeStruct((b, s, ATTN_WIDTH), BF16),
        scratch_shapes=[pltpu.VMEM((s, hd), BF16)] * 3,
        compiler_params=_params("parallel", "arbitrary"),
        name="attention",
    )(proj, proj, proj, gq.reshape(1, hd), gk.reshape(1, hd), cos, sin_signed, bias,
      g_out.reshape(1, ATTN_WIDTH))


def _shortconv_eo(ue, uo, w_ref, b_ref):
    half, edge = ue.shape[0], 8
    w0, w1, w2, b = w_ref[0:1, :], w_ref[1:2, :], w_ref[2:3, :], b_ref[...]
    taps = lambda prev, cur, nxt: w0 * prev + w1 * cur + w2 * nxt + b
    uo_prev = pltpu.roll(uo, 1, axis=0)
    ue_next = pltpu.roll(ue, half - 1, axis=0)
    row = lax.broadcasted_iota(jnp.int32, (edge, ue.shape[1]), 0)
    lo, hi = slice(0, edge), slice(half - edge, half)
    even_top = taps(jnp.where(row == 0, 0.0, uo_prev[lo]), ue[lo], uo[lo])
    odd_bot = taps(ue[hi], uo[hi], jnp.where(row == edge - 1, 0.0, ue_next[hi]))
    even = jnp.concatenate([even_top, taps(uo_prev, ue, uo)[edge:]], axis=0)
    odd = jnp.concatenate([taps(ue, uo, ue_next)[:half - edge], odd_bot], axis=0)
    return even, odd


def _filt_hidden_body(emb_ref, w1_ref, b1_ref, w2_ref, b2_ref, fr_ref, o_ref):
    hp = lax.Precision.HIGHEST
    h = jnp.dot(emb_ref[...], w1_ref[...], precision=hp, preferred_element_type=F32)
    h = jnp.sin(fr_ref[0:1, :] * (h + b1_ref[...]))
    h = jnp.dot(h, w2_ref[...], precision=hp, preferred_element_type=F32)
    o_ref[...] = jnp.sin(fr_ref[1:2, :] * (h + b2_ref[...]))


def _filt_out_body(h_ref, w3_ref, t_ref, delta_ref, o_ref):
    h = jnp.dot(h_ref[...], w3_ref[...], precision=lax.Precision.HIGHEST,
                preferred_element_type=F32)
    o_ref[...] = h * jnp.exp(-t_ref[...] * delta_ref[...])


def _pad_to(a, shape):
    return jnp.pad(a, [(0, t - s) for s, t in zip(a.shape, shape)])


def _hyena_filters(w1, b1, w2, b2, freq, w3, tn=512):
    seq = SEQ
    t = jnp.linspace(0.0, 1.0, seq, dtype=F32)[:, None]
    bands = (FILTER_EMB_DIM - 1) // 2
    f = jnp.linspace(1e-4, bands - 1, bands, dtype=F32)[None, :]
    wpos = (2.0 * math.pi) * jnp.arange(seq, dtype=F32)[:, None] / seq
    emb = jnp.concatenate([t, jnp.cos(f * wpos), -jnp.sin(f * wpos)], axis=-1)
    deltas = jnp.abs(jnp.linspace(math.log(DECAY_FAST) / DECAY_TARGET,
                                  math.log(DECAY_SLOW) / DECAY_TARGET, HYENA_WIDTH, dtype=F32))
    ncol = w3.shape[1]
    deltas = jnp.tile(deltas, ncol // HYENA_WIDTH).reshape(1, ncol)
    hid = LANES
    hidden = pl.pallas_call(
        _filt_hidden_body,
        out_shape=jax.ShapeDtypeStruct((seq, hid), F32),
        name="filt_hidden",
    )(_pad_to(emb, (seq, hid)), _pad_to(w1, (hid, hid)), _pad_to(b1[None], (1, hid)),
      _pad_to(w2, (hid, hid)), _pad_to(b2[None], (1, hid)), _pad_to(freq, (2, hid)))
    return pl.pallas_call(
        _filt_out_body,
        grid=(ncol // tn,),
        in_specs=[pl.BlockSpec((seq, hid), lambda j: (0, 0)),
                  pl.BlockSpec((hid, tn), lambda j: (0, j)),
                  pl.BlockSpec((seq, 1), lambda j: (0, 0)),
                  pl.BlockSpec((1, tn), lambda j: (0, j))],
        out_specs=pl.BlockSpec((seq, tn), lambda j: (0, j)),
        out_shape=jax.ShapeDtypeStruct((seq, ncol), F32),
        compiler_params=_params("parallel"),
        name="filt_out",
    )(hidden, _pad_to(w3, (hid, ncol)), t, deltas)


def _dft_matrices():
    h = DFT_HALF
    n = jnp.arange(SEQ, dtype=jnp.int32)
    def table(t):
        ang = ((t[:, None] * n[None, :]) & (DFT_N - 1)).astype(F32) * (2.0 * math.pi / DFT_N)
        return jnp.cos(ang), jnp.sin(ang)
    c_hi, s_hi = (x[:, None, :] for x in table(jnp.arange(0, h, DFT_SPLIT, dtype=jnp.int32)))
    c_lo, s_lo = (x[None, :, :] for x in table(jnp.arange(DFT_SPLIT, dtype=jnp.int32)))
    c = (c_hi * c_lo - s_hi * s_lo).reshape(h, SEQ)
    s = (s_hi * c_lo + c_hi * s_lo).reshape(h, SEQ)
    ce, co, se, so = c[:, 0::2], c[:, 1::2], -s[:, 0::2], -s[:, 1::2]
    alt = jnp.where(jnp.arange(h) % 2 == 0, 1.0, -1.0).astype(F32)
    first = jnp.arange(h) == 0
    fwd = [ce, co, jnp.where(first[:, None], alt[None, :], se),
           jnp.where(first[:, None], -alt[None, :], so)]
    wgt = jnp.where(first, 1.0 / DFT_N, 2.0 / DFT_N)[None, :]
    inv = [ce.T * wgt, jnp.where(first[None, :], alt[:, None] / DFT_N, se.T * wgt),
           co.T * wgt, jnp.where(first[None, :], -alt[:, None] / DFT_N, so.T * wgt)]
    return jnp.stack(fwd).astype(BF16), jnp.stack(inv).astype(BF16)


def _rdft(mats_ref, ze, zo):
    ze = ze.astype(BF16)
    zo = zo.astype(BF16)
    return (_dot(mats_ref[0], ze), _dot(mats_ref[1], zo), _dot(mats_ref[2], ze),
            _dot(mats_ref[3], zo))


def _first_row(shape):
    return lax.broadcasted_iota(jnp.int32, shape, 0) == 0


def _parity_view(a):
    return a.reshape(a.shape[:-2] + (a.shape[-2] // 2, 2 * a.shape[-1]))


def _parity_specs(lead, width, col, tn, index):
    def spec(parity):
        def index_map(*g):
            pre, j = index(*g)
            return pre + (0, (col + parity * width) // tn + j)
        return pl.BlockSpec((None,) * lead + (DFT_HALF, tn), index_map)
    return [spec(0), spec(1)]


def _kspec_body(mats_ref, hfe_ref, hfo_ref, hbe_ref, hbo_ref, hb0_ref, kre_ref, kim_ref):
    h = DFT_HALF
    fer, for_, fei, foi = _rdft(mats_ref, hfe_ref[...], hfo_ref[...])
    ber, bor, bei, boi = _rdft(mats_ref, hbe_ref[...], hbo_ref[...])
    hb0 = hb0_ref[0:1, :]
    kre_ref[0:h, :] = (fer + for_) + (ber + bor) - hb0
    kre_ref[h:, :] = (fer - for_) + (ber - bor) - hb0
    kim_ref[0:h, :] = (fei + foi) - (bei + boi)
    kim_ref[h:, :] = (foi - fei) - (boi - bei)
    r = 8
    first = _first_row((r, fer.shape[1]))
    kim_ref[0:r, :] = jnp.where(first, fei[:r] + bei[:r] - hb0, (fei + foi - bei - boi)[:r])
    kim_ref[h:h + r, :] = jnp.where(first, foi[:r] - boi[:r], (foi - fei - boi + bei)[:r])


def _kspec(fwd, filt, tn=256):
    c = HYENA_WIDTH
    nc = c // tn
    width = filt.shape[1]
    filt2 = _parity_view(filt)
    out = jax.ShapeDtypeStruct((SEQ, HYENA_ORDER * c), F32)
    index = lambda o, j: ((), o * 2 * nc + j)
    return pl.pallas_call(
        _kspec_body,
        grid=(HYENA_ORDER, nc),
        in_specs=[pl.BlockSpec(fwd.shape, lambda o, j: (0, 0, 0), pipeline_mode=pl.Buffered(1))]
        + _parity_specs(0, width, 0, tn, index) + _parity_specs(0, width, c, tn, index)
        + [pl.BlockSpec((8, tn), lambda o, j: (0, o * 2 * nc + nc + j))],
        out_specs=[pl.BlockSpec((SEQ, tn), lambda o, j: (0, o * nc + j))] * 2,
        out_shape=[out, out],
        compiler_params=_params("parallel", "arbitrary"),
        name="kspec",
    )(fwd, filt2, filt2, filt2, filt2, filt2)


def _dft_fwd_body(*refs, conv):
    if conv:
        mats_ref, ze_ref, zo_ref, cw_ref, cb_ref, kre_ref, kim_ref, yre_ref, yim_ref = refs
        ze, zo = _shortconv_eo(ze_ref[...], zo_ref[...], cw_ref, cb_ref)
    else:
        mats_ref, ze_ref, zo_ref, kre_ref, kim_ref, yre_ref, yim_ref = refs
        ze, zo = ze_ref[...], zo_ref[...]
    h = DFT_HALF
    er, or_, ei, oi = _rdft(mats_ref, ze, zo)
    out = yre_ref.dtype
    for rows, zr, zi in ((slice(0, h), er + or_, ei + oi), (slice(h, 2 * h), er - or_, oi - ei)):
        kr, ki = kre_ref[rows, :], kim_ref[rows, :]
        yre_ref[rows, :] = (zr * kr - zi * ki).astype(out)
        yim_ref[rows, :] = (zr * ki + zi * kr).astype(out)
    r = 16
    first = _first_row((r, er.shape[1]))
    e, o, p, q = er[:r], or_[:r], ei[:r], oi[:r]
    krt, krb, kit, kib = kre_ref[0:r, :], kre_ref[h:h + r, :], kim_ref[0:r, :], kim_ref[h:h + r, :]
    re4 = p * kit - q * kib
    im4 = p * kib + q * kit
    yre_ref[0:r, :] = jnp.where(first, (e + o) * krt, (e + o) * krt - (p + q) * kit).astype(out)
    yim_ref[0:r, :] = jnp.where(first, re4 + im4, (e + o) * kit + (p + q) * krt).astype(out)
    yre_ref[h:h + r, :] = jnp.where(first, (e - o) * krb, (e - o) * krb - (q - p) * kib).astype(out)
    yim_ref[h:h + r, :] = jnp.where(first, im4 - re4, (e - o) * kib + (q - p) * krb).astype(out)


def _seq_specs(seq_operand, tn, index):
    if len(seq_operand) == 2:
        spec = pl.BlockSpec((None, DFT_HALF, tn), lambda *g: index(*g)[0] + (0, index(*g)[1]))
        return [spec, spec], list(seq_operand)
    view, width, col = seq_operand
    return _parity_specs(1, width, col, tn, index), [view, view]


def _dft_fwd(fwd, z, conv, kre, kim, order, tn=512):
    b = z[0].shape[0]
    c = HYENA_WIDTH
    nc = c // tn
    out = jax.ShapeDtypeStruct((b, SEQ, c), BF16)
    z_specs, z_ops = _seq_specs(z, tn, lambda j, i: ((i,), j))
    in_specs = [pl.BlockSpec(fwd.shape, lambda j, i: (0, 0, 0), pipeline_mode=pl.Buffered(1))]
    in_specs += z_specs
    operands = [fwd] + z_ops
    if conv is not None:
        cw, cb, col = conv
        in_specs += [pl.BlockSpec((cw.shape[0], tn), lambda j, i: (0, col // tn + j)),
                     pl.BlockSpec((1, tn), lambda j, i: (0, col // tn + j))]
        operands += [cw, cb]
    in_specs += [pl.BlockSpec((SEQ, tn), lambda j, i: (0, order * nc + j),
                              pipeline_mode=pl.Buffered(1))] * 2
    operands += [kre, kim]
    return pl.pallas_call(
        functools.partial(_dft_fwd_body, conv=conv is not None),
        grid=(nc, b),
        in_specs=in_specs,
        out_specs=[pl.BlockSpec((None, SEQ, tn), lambda j, i: (i, 0, j))] * 2,
        out_shape=[out, out],
        compiler_params=_params("parallel", "arbitrary"),
        name="dft_fwd",
    )(*operands)


def _dft_inv_body(*refs, conv_z, final):
    mats_ref, yre_ref, yim_ref, ze_ref, zo_ref, *refs = refs
    ze, zo = ze_ref[...], zo_ref[...]
    if conv_z:
        zw_ref, zb_ref, *refs = refs
        ze, zo = _shortconv_eo(ze, zo, zw_ref, zb_ref)
    ge_ref, go_ref, gw_ref, gb_ref, b_ref, gn_ref, *outs = refs
    h = DFT_HALF
    ge, go = _shortconv_eo(ge_ref[...], go_ref[...], gw_ref, gb_ref)
    yrt, yrb = yre_ref[0:h, :].astype(F32), yre_ref[h:, :].astype(F32)
    yit, yib = yim_ref[0:h, :].astype(F32), yim_ref[h:, :].astype(F32)
    ye = _dot(mats_ref[0], (yrt + yrb).astype(BF16)) + _dot(mats_ref[1], (yit - yib).astype(BF16))
    yo = _dot(mats_ref[2], (yrt - yrb).astype(BF16)) + _dot(mats_ref[3], (yit + yib).astype(BF16))
    bias = b_ref[...]
    halves = (ge * (ye + bias * ze), go * (yo + bias * zo))
    if not final:
        for o_ref, zn in zip(outs, halves):
            o_ref[...] = zn
        return
    o_ref, stage_ref = outs
    for gi in range(halves[0].shape[1] // HYENA_GROUP):
        sl = slice(gi * HYENA_GROUP, (gi + 1) * HYENA_GROUP)
        for parity, zn in enumerate(halves):
            blk = zn[:, sl]
            blk = blk * lax.rsqrt(jnp.mean(blk * blk, axis=-1, keepdims=True) + RMS_EPS)
            stage_ref[pl.ds(parity, h, stride=2), :] = blk * gn_ref[:, sl]
        o_ref[:, sl] = stage_ref[...].astype(o_ref.dtype)


def _dft_inv(inv, yre, yim, z, conv_z, gate, conv_g, bias, gnorm, final, tn=512):
    b = yre.shape[0]
    c = HYENA_WIDTH
    index = lambda j, i: ((i,), j)
    taps = lambda cw, col: [pl.BlockSpec((cw.shape[0], tn), lambda j, i: (0, col // tn + j)),
                            pl.BlockSpec((1, tn), lambda j, i: (0, col // tn + j))]
    z_specs, z_ops = _seq_specs(z, tn, index)
    g_specs, g_ops = _seq_specs(gate, tn, index)
    in_specs = [pl.BlockSpec(inv.shape, lambda j, i: (0, 0, 0), pipeline_mode=pl.Buffered(1)),
                pl.BlockSpec((None, SEQ, tn), lambda j, i: (i, 0, j)),
                pl.BlockSpec((None, SEQ, tn), lambda j, i: (i, 0, j))] + z_specs
    operands = [inv, yre, yim] + z_ops
    if conv_z is not None:
        in_specs += taps(conv_z[0], conv_z[2])
        operands += list(conv_z[:2])
    in_specs += g_specs + taps(conv_g[0], conv_g[2])
    in_specs += [pl.BlockSpec((1, tn), lambda j, i: (0, j))] * 2
    operands += g_ops + [conv_g[0], conv_g[1], bias.reshape(1, c), gnorm.reshape(1, c)]
    if final:
        out_specs = pl.BlockSpec((None, SEQ, tn), lambda j, i: (i, 0, j))
        out_shape = jax.ShapeDtypeStruct((b, SEQ, c), BF16)
        scratch = [pltpu.VMEM((SEQ, HYENA_GROUP), F32)]
    else:
        out_specs = [pl.BlockSpec((None, DFT_HALF, tn), lambda j, i: (i, 0, j))] * 2
        out_shape = [jax.ShapeDtypeStruct((b, DFT_HALF, c), F32)] * 2
        scratch = []
    return pl.pallas_call(
        functools.partial(_dft_inv_body, conv_z=conv_z is not None, final=final),
        grid=(c // tn, b),
        in_specs=in_specs,
        out_specs=out_specs,
        out_shape=out_shape,
        scratch_shapes=scratch,
        compiler_params=_params("parallel", "arbitrary"),
        name="dft_inv",
    )(*operands)


def _hyena(proj, dft, conv_w, conv_b, w1, b1, w2, b2, freq, w3, filt_bias, g_out):
    fwd, inv = dft
    c = HYENA_WIDTH
    raw = IN_COLS - (HYENA_ORDER + 1) * c
    conv_b = conv_b.reshape(1, -1)
    proj2 = _parity_view(proj)
    filt = _hyena_filters(w1, b1, w2, b2, freq, w3)
    kre, kim = _kspec(fwd, filt)
    z, conv_z = (proj2, IN_COLS, raw), (conv_w, conv_b, 0)
    for order in range(HYENA_ORDER):
        final = order == HYENA_ORDER - 1
        yre, yim = _dft_fwd(fwd, z, conv_z, kre, kim, order)
        z = _dft_inv(inv, yre, yim, z, conv_z, (proj2, IN_COLS, raw + (order + 1) * c),
                     (conv_w, conv_b, (order + 1) * c), filt_bias[order], g_out, final)
        conv_z = None
    return z


def _rotary_tables():
    half = HEAD_DIM // 2
    pos = jnp.arange(SEQ, dtype=F32)
    inv = ROPE_THETA ** (-jnp.arange(half, dtype=F32) / half)
    ang = pos[:, None] * inv[None, :]
    cos = jnp.cos(ang)
    sin = jnp.sin(ang)
    return jnp.concatenate([cos, cos], axis=-1), jnp.concatenate([-sin, sin], axis=-1)


def _mixer(proj, dft, rot, q_norm, k_norm, conv_w, conv_b, w1, b1, w2, b2, freq, w3,
           filt_bias, attn_out_norm, hyena_out_norm):
    attn = _attention(proj, q_norm, k_norm, rot[0], rot[1], attn_out_norm)
    hy = _hyena(proj, dft, conv_w, conv_b, w1, b1, w2, b2, freq, w3, filt_bias, hyena_out_norm)
    return attn, hy


def _ffn(x, g, w_gate, w_up, w_down, idx):
    a, wd = _matmul(_swiglu_cast_body, [_rmsnorm(x, g)], [(w_gate, idx), (w_up, idx)], None, BF16,
                    2048, 256, "ffn_up", cast_rows=(w_down, idx))
    return _matmul(functools.partial(_mm_res_body, scale=0.5), [a], [(wd, ())], x, F32,
                   1024, 256, "ffn_down")


def kernel(x, ffn_norm, ffn_w_gate, ffn_w_up, ffn_w_down, mix_norm, w_in, q_norm, k_norm, conv_w, conv_b, filt_w1, filt_b1, filt_w2, filt_b2, filt_freq, filt_w3, filt_bias, attn_out_norm, hyena_out_norm, w_out):
    b, s, d = x.shape
    x = x.reshape(b * s, d)
    dft = _dft_matrices()
    rot = _rotary_tables()
    for l in range(DEPTH):
        x = _ffn(x, ffn_norm[l, 0], ffn_w_gate, ffn_w_up, ffn_w_down, (l, 0))
        proj = _matmul(_mm_body, [_rmsnorm(x, mix_norm[l])], [(w_in, (l,))], None, F32,
                       2048, 256, "in_proj")
        attn, hy = _mixer(proj.reshape(b, s, IN_COLS), dft, rot, q_norm[l], k_norm[l], conv_w[l],
                          conv_b[l], filt_w1[l], filt_b1[l], filt_w2[l], filt_b2[l], filt_freq[l],
                          filt_w3[l], filt_bias[l], attn_out_norm[l], hyena_out_norm[l])
        x = _matmul(_mm2_res_body, [attn.reshape(b * s, -1), hy.reshape(b * s, -1)],
                    [(w_out, (l,))], x, F32, 2048, 256, "out_proj")
        x = _ffn(x, ffn_norm[l, 1], ffn_w_gate, ffn_w_up, ffn_w_down, (l, 1))
    return x.reshape(b, s, d)
```

```python
import functools
import math

import numpy as np
import jax
import jax.numpy as jnp
from jax import lax
from jax.experimental import pallas as pl
from jax.experimental.pallas import tpu as pltpu

D_MODEL = 4096
BATCH = 4
SEQ = 2048
DEPTH = 2
ATTN_WIDTH = D_MODEL // 2
HYENA_WIDTH = D_MODEL - ATTN_WIDTH
HEAD_DIM = 128
N_HEADS = ATTN_WIDTH // HEAD_DIM
DILATED_CONFIGS = ((128, 1), (512, 4), (2048, 16))
ROPE_THETA = 10000.0
HYENA_ORDER = 2
HYENA_GROUP = 128
FILTER_EMB_DIM = 33
FILTER_HIDDEN = 64
DECAY_FAST = 0.3
DECAY_SLOW = 1.5
DECAY_TARGET = 1e-2
FFN_HIDDEN = 256 * (-(-(8 * D_MODEL) // (3 * 256)))
IN_COLS = 3 * ATTN_WIDTH + (HYENA_ORDER + 1) * HYENA_WIDTH
RMS_EPS = 1e-6
NEG_INF = -1e30

F32 = jnp.float32
BF16 = jnp.bfloat16

V7X_VMEM_LIMIT_BYTES = 56 * 1024 * 1024
LANES = 128
ATTN_TQ = 256
ATTN_REACH = max((w // (2 * d)) * d for w, d in DILATED_CONFIGS)
DFT_N = 2 * SEQ
DFT_HALF = SEQ // 2
DFT_SPLIT = 32


def _params(*sem):
    return pltpu.CompilerParams(dimension_semantics=sem, vmem_limit_bytes=V7X_VMEM_LIMIT_BYTES)


def _dot(a, b):
    return jnp.dot(a, b, preferred_element_type=F32)


def _rmsnorm_body(x_ref, g_ref, o_ref):
    x = x_ref[...]
    ms = jnp.mean(x * x, axis=-1, keepdims=True)
    o_ref[...] = (x * lax.rsqrt(ms + RMS_EPS) * g_ref[...]).astype(o_ref.dtype)


def _rmsnorm(x, g, tm=512):
    m, d = x.shape
    return pl.pallas_call(
        _rmsnorm_body,
        grid=(m // tm,),
        in_specs=[pl.BlockSpec((tm, d), lambda i: (i, 0)),
                  pl.BlockSpec((1, d), lambda i: (0, 0))],
        out_specs=pl.BlockSpec((tm, d), lambda i: (i, 0)),
        out_shape=jax.ShapeDtypeStruct((m, d), BF16),
        compiler_params=_params("parallel"),
        name="rmsnorm",
    )(x, g.reshape(1, d))


def _swiglu_body(a_ref, wg_ref, wu_ref, o_ref):
    a = a_ref[...]
    g = _dot(a, wg_ref[...].astype(BF16))
    u = _dot(a, wu_ref[...].astype(BF16))
    o_ref[...] = (g * jax.nn.sigmoid(g) * u).astype(o_ref.dtype)


def _mm_body(a_ref, w_ref, o_ref):
    o_ref[...] = _dot(a_ref[...], w_ref[...].astype(BF16)).astype(o_ref.dtype)


def _mm_res_body(a_ref, w_ref, r_ref, o_ref, *, scale):
    o_ref[...] = r_ref[...] + scale * _dot(a_ref[...], w_ref[...].astype(BF16))


def _mm2_res_body(a1_ref, a2_ref, w_ref, r_ref, o_ref):
    k1 = a1_ref.shape[1]
    acc = _dot(a1_ref[...], w_ref[0:k1, :].astype(BF16))
    acc += _dot(a2_ref[...], w_ref[k1:, :].astype(BF16))
    o_ref[...] = r_ref[...] + acc


def _swiglu_cast_body(a_ref, wg_ref, wu_ref, wd_ref, o_ref, wdb_ref):
    _swiglu_body(a_ref, wg_ref, wu_ref, o_ref)

    @pl.when(pl.program_id(0) == 0)
    def _():
        wdb_ref[...] = wd_ref[...].astype(wdb_ref.dtype)


def _matmul(body, a_list, weights, res, out_dtype, tm, tn, name, cast_rows=None):
    m = a_list[0].shape[0]
    n = weights[0][0].shape[-1]
    k = sum(a.shape[1] for a in a_list)
    nj = n // tn
    in_specs = [pl.BlockSpec((tm, a.shape[1]), lambda i, j: (i, 0), pipeline_mode=pl.Buffered(1))
                for a in a_list]
    for w, prefix in weights:
        in_specs.append(pl.BlockSpec((None,) * len(prefix) + (k, tn),
                                     lambda i, j, prefix=prefix: prefix + (0, j)))
    operands = list(a_list) + [w for w, _ in weights]
    if res is not None:
        in_specs.append(pl.BlockSpec((tm, tn), lambda i, j: (i, j)))
        operands.append(res)
    out_specs = pl.BlockSpec((tm, tn), lambda i, j: (i, j))
    out_shape = jax.ShapeDtypeStruct((m, n), out_dtype)
    if cast_rows is not None:
        w, prefix = cast_rows
        cols = w.shape[-1]
        row_block = lambda i, j: jnp.where(i == 0, j, nj - 1)
        in_specs.append(pl.BlockSpec((None,) * len(prefix) + (tn, cols),
                                     lambda i, j: prefix + (row_block(i, j), 0)))
        operands.append(w)
        out_specs = [out_specs, pl.BlockSpec((tn, cols), lambda i, j: (row_block(i, j), 0))]
        out_shape = [out_shape, jax.ShapeDtypeStruct((n, cols), BF16)]
    return pl.pallas_call(
        body,
        grid=(m // tm, nj),
        in_specs=in_specs,
        out_specs=out_specs,
        out_shape=out_shape,
        compiler_params=_params("arbitrary", "arbitrary"),
        name=name,
    )(*operands)


def _attn_bias_table():
    il = np.arange(ATTN_TQ)[:, None]
    c = np.arange(ATTN_TQ + 2 * ATTN_REACH)[None, :]
    d = il + ATTN_REACH - c
    mult = np.zeros(d.shape, np.int64)
    for window, dil in DILATED_CONFIGS:
        half = window // (2 * dil)
        mult += ((d % dil) == 0) & (np.abs(d) <= half * dil)
    return np.where(mult > 0, np.log(np.maximum(mult, 1)), NEG_INF).astype(np.float32)


def _attn_body(q_ref, k_ref, v_ref, gq_ref, gk_ref, cos_ref, sin_ref, bias_ref, go_ref,
               o_ref, qs_ref, ks_ref, vs_ref):
    seq = q_ref.shape[1]
    cos = cos_ref[...]
    sin = sin_ref[...]

    def norm_rot(x, g):
        y = x * lax.rsqrt(jnp.mean(x * x, axis=-1, keepdims=True) + RMS_EPS) * g
        return y * cos + pltpu.roll(y, HEAD_DIM // 2, axis=1) * sin

    qs_ref[...] = (norm_rot(q_ref[0], gq_ref[...]) * (1.0 / math.sqrt(HEAD_DIM))).astype(BF16)
    ks_ref[...] = norm_rot(k_ref[0], gk_ref[...]).astype(BF16)
    vs_ref[...] = v_ref[0].astype(BF16)

    for t in range(seq // ATTN_TQ):
        q0 = t * ATTN_TQ
        lo = max(0, q0 - ATTN_REACH)
        hi = min(seq, q0 + ATTN_TQ + ATTN_REACH)
        c0 = lo - (q0 - ATTN_REACH)
        s = lax.dot_general(qs_ref[q0:q0 + ATTN_TQ, :], ks_ref[lo:hi, :],
                            (((1,), (1,)), ((), ())), preferred_element_type=F32)
        s = s + bias_ref[:, c0:c0 + (hi - lo)]
        m = jnp.max(s, axis=-1, keepdims=True)
        p = jnp.exp(s - m)
        den = jnp.sum(p, axis=-1, keepdims=True)
        o = _dot(p.astype(BF16), vs_ref[lo:hi, :]) / den
        o = o * lax.rsqrt(jnp.mean(o * o, axis=-1, keepdims=True) + RMS_EPS) * go_ref[...]
        o_ref[0, q0:q0 + ATTN_TQ, :] = o.astype(o_ref.dtype)


def _attention(proj, gq, gk, cos, sin_signed, g_out):
    b, s, _ = proj.shape
    hd = HEAD_DIM
    bias = jnp.asarray(_attn_bias_table())
    head = lambda off: pl.BlockSpec((1, s, hd), lambda i, h: (i, 0, off + h))
    const2 = lambda shape: pl.BlockSpec(shape, lambda i, h: (0, 0))
    return pl.pallas_call(
        _attn_body,
        grid=(b, N_HEADS),
        in_specs=[head(0), head(N_HEADS), head(2 * N_HEADS),
                  const2((1, hd)), const2((1, hd)),
                  const2((s, hd)), const2((s, hd)),
                  const2(bias.shape),
                  pl.BlockSpec((1, hd), lambda i, h: (0, h))],
        out_specs=pl.BlockSpec((1, s, hd), lambda i, h: (i, 0, h)),
        out_shape=jax.ShapeDtypeStruct((b, s, ATTN_WIDTH), BF16),
        scratch_shapes=[pltpu.VMEM((s, hd), BF16)] * 3,
        compiler_params=_params("parallel", "arbitrary"),
        name="attention",
    )(proj, proj, proj, gq.reshape(1, hd), gk.reshape(1, hd), cos, sin_signed, bias,
      g_out.reshape(1, ATTN_WIDTH))


def _shortconv_eo(ue, uo, w_ref, b_ref):
    half, edge = ue.shape[0], 8
    w0, w1, w2, b = w_ref[0:1, :], w_ref[1:2, :], w_ref[2:3, :], b_ref[...]
    taps = lambda prev, cur, nxt: w0 * prev + w1 * cur + w2 * nxt + b
    uo_prev = pltpu.roll(uo, 1, axis=0)
    ue_next = pltpu.roll(ue, half - 1, axis=0)
    row = lax.broadcasted_iota(jnp.int32, (edge, ue.shape[1]), 0)
    lo, hi = slice(0, edge), slice(half - edge, half)
    even_top = taps(jnp.where(row == 0, 0.0, uo_prev[lo]), ue[lo], uo[lo])
    odd_bot = taps(ue[hi], uo[hi], jnp.where(row == edge - 1, 0.0, ue_next[hi]))
    even = jnp.concatenate([even_top, taps(uo_prev, ue, uo)[edge:]], axis=0)
    odd = jnp.concatenate([taps(ue, uo, ue_next)[:half - edge], odd_bot], axis=0)
    return even, odd


def _filt_hidden_body(emb_ref, w1_ref, b1_ref, w2_ref, b2_ref, fr_ref, o_ref):
    hp = lax.Precision.HIGHEST
    h = jnp.dot(emb_ref[...], w1_ref[...], precision=hp, preferred_element_type=F32)
    h = jnp.sin(fr_ref[0:1, :] * (h + b1_ref[...]))
    h = jnp.dot(h, w2_ref[...], precision=hp, preferred_element_type=F32)
    o_ref[...] = jnp.sin(fr_ref[1:2, :] * (h + b2_ref[...]))


def _filt_out_body(h_ref, w3_ref, t_ref, delta_ref, o_ref):
    h = jnp.dot(h_ref[...], w3_ref[...], precision=lax.Precision.HIGHEST,
                preferred_element_type=F32)
    o_ref[...] = h * jnp.exp(-t_ref[...] * delta_ref[...])


def _pad_to(a, shape):
    return jnp.pad(a, [(0, t - s) for s, t in zip(a.shape, shape)])


def _hyena_filters(w1, b1, w2, b2, freq, w3, tn=512):
    seq = SEQ
    t = jnp.linspace(0.0, 1.0, seq, dtype=F32)[:, None]
    bands = (FILTER_EMB_DIM - 1) // 2
    f = jnp.linspace(1e-4, bands - 1, bands, dtype=F32)[None, :]
    wpos = (2.0 * math.pi) * jnp.arange(seq, dtype=F32)[:, None] / seq
    emb = jnp.concatenate([t, jnp.cos(f * wpos), -jnp.sin(f * wpos)], axis=-1)
    deltas = jnp.abs(jnp.linspace(math.log(DECAY_FAST) / DECAY_TARGET,
                                  math.log(DECAY_SLOW) / DECAY_TARGET, HYENA_WIDTH, dtype=F32))
    ncol = w3.shape[1]
    deltas = jnp.tile(deltas, ncol // HYENA_WIDTH).reshape(1, ncol)
    hid = LANES
    hidden = pl.pallas_call(
        _filt_hidden_body,
        out_shape=jax.ShapeDtypeStruct((seq, hid), F32),
        name="filt_hidden",
    )(_pad_to(emb, (seq, hid)), _pad_to(w1, (hid, hid)), _pad_to(b1[None], (1, hid)),
      _pad_to(w2, (hid, hid)), _pad_to(b2[None], (1, hid)), _pad_to(freq, (2, hid)))
    return pl.pallas_call(
        _filt_out_body,
        grid=(ncol // tn,),
        in_specs=[pl.BlockSpec((seq, hid), lambda j: (0, 0)),
                  pl.BlockSpec((hid, tn), lambda j: (0, j)),
                  pl.BlockSpec((seq, 1), lambda j: (0, 0)),
                  pl.BlockSpec((1, tn), lambda j: (0, j))],
        out_specs=pl.BlockSpec((seq, tn), lambda j: (0, j)),
        out_shape=jax.ShapeDtypeStruct((seq, ncol), F32),
        compiler_params=_params("parallel"),
        name="filt_out",
    )(hidden, _pad_to(w3, (hid, ncol)), t, deltas)


def _dft_matrices():
    h = DFT_HALF
    n = jnp.arange(SEQ, dtype=jnp.int32)
    def table(t):
        ang = ((t[:, None] * n[None, :]) & (DFT_N - 1)).astype(F32) * (2.0 * math.pi / DFT_N)
        return jnp.cos(ang), jnp.sin(ang)
    c_hi, s_hi = (x[:, None, :] for x in table(jnp.arange(0, h, DFT_SPLIT, dtype=jnp.int32)))
    c_lo, s_lo = (x[None, :, :] for x in table(jnp.arange(DFT_SPLIT, dtype=jnp.int32)))
    c = (c_hi * c_lo - s_hi * s_lo).reshape(h, SEQ)
    s = (s_hi * c_lo + c_hi * s_lo).reshape(h, SEQ)
    ce, co, se, so = c[:, 0::2], c[:, 1::2], -s[:, 0::2], -s[:, 1::2]
    alt = jnp.where(jnp.arange(h) % 2 == 0, 1.0, -1.0).astype(F32)
    first = jnp.arange(h) == 0
    fwd = [ce, co, jnp.where(first[:, None], alt[None, :], se),
           jnp.where(first[:, None], -alt[None, :], so)]
    wgt = jnp.where(first, 1.0 / DFT_N, 2.0 / DFT_N)[None, :]
    inv = [ce.T * wgt, jnp.where(first[None, :], alt[:, None] / DFT_N, se.T * wgt),
           co.T * wgt, jnp.where(first[None, :], -alt[:, None] / DFT_N, so.T * wgt)]
    return jnp.stack(fwd).astype(BF16), jnp.stack(inv).astype(BF16)


def _rdft(mats_ref, ze, zo):
    ze = ze.astype(BF16)
    zo = zo.astype(BF16)
    return (_dot(mats_ref[0], ze), _dot(mats_ref[1], zo), _dot(mats_ref[2], ze),
            _dot(mats_ref[3], zo))


def _first_row(shape):
    return lax.broadcasted_iota(jnp.int32, shape, 0) == 0


def _seq_specs(operand, lead, tn, index):
    def spec(rows, width, offset):
        def index_map(*g):
            pre, j = index(*g)
            return pre + (0, offset + j * (tn // width))
        return pl.BlockSpec((None,) * lead + (rows, width), index_map)
    if not _is_natural(operand):
        return [spec(DFT_HALF, tn, 0)] * 2, list(operand)
    array, col = operand
    groups = tn // LANES
    return [spec(SEQ, LANES, col // LANES + g) for g in range(groups)], [array] * groups


def _load_seq(refs, natural):
    if not natural:
        return refs[0][...], refs[1][...]
    parity = lambda p: jnp.concatenate([r[pl.ds(p, DFT_HALF, stride=2), :] for r in refs], axis=1)
    return parity(0), parity(1)


def _kspec_body(mats_ref, *refs):
    h = DFT_HALF
    *h_refs, hb0_ref, kre_ref, kim_ref = refs
    n = len(h_refs) // 2
    fer, for_, fei, foi = _rdft(mats_ref, *_load_seq(h_refs[:n], True))
    ber, bor, bei, boi = _rdft(mats_ref, *_load_seq(h_refs[n:], True))
    hb0 = hb0_ref[0:1, :]
    kre_ref[0:h, :] = (fer + for_) + (ber + bor) - hb0
    kre_ref[h:, :] = (fer - for_) + (ber - bor) - hb0
    kim_ref[0:h, :] = (fei + foi) - (bei + boi)
    kim_ref[h:, :] = (foi - fei) - (boi - bei)
    r = 8
    first = _first_row((r, fer.shape[1]))
    kim_ref[0:r, :] = jnp.where(first, fei[:r] + bei[:r] - hb0, (fei + foi - bei - boi)[:r])
    kim_ref[h:h + r, :] = jnp.where(first, foi[:r] - boi[:r], (foi - fei - boi + bei)[:r])


def _kspec(fwd, filt, tn=256):
    c = HYENA_WIDTH
    nc = c // tn
    out = jax.ShapeDtypeStruct((SEQ, HYENA_ORDER * c), F32)
    index = lambda o, j: ((), o * 2 * nc + j)
    hf_specs, hf_ops = _seq_specs((filt, 0), 0, tn, index)
    hb_specs, hb_ops = _seq_specs((filt, c), 0, tn, index)
    return pl.pallas_call(
        _kspec_body,
        grid=(HYENA_ORDER, nc),
        in_specs=[pl.BlockSpec(fwd.shape, lambda o, j: (0, 0, 0), pipeline_mode=pl.Buffered(1))]
        + hf_specs + hb_specs + [pl.BlockSpec((8, tn), lambda o, j: (0, o * 2 * nc + nc + j))],
        out_specs=[pl.BlockSpec((SEQ, tn), lambda o, j: (0, o * nc + j))] * 2,
        out_shape=[out, out],
        compiler_params=_params("parallel", "arbitrary"),
        name="kspec",
    )(fwd, *hf_ops, *hb_ops, filt)


def _dft_fwd_body(mats_ref, *refs, nz, natural, conv):
    ze, zo = _load_seq(refs[:nz], natural)
    if conv:
        ze, zo = _shortconv_eo(ze, zo, *refs[nz:nz + 2])
    kre_ref, kim_ref, yre_ref, yim_ref = refs[-4:]
    h = DFT_HALF
    er, or_, ei, oi = _rdft(mats_ref, ze, zo)
    out = yre_ref.dtype
    for rows, zr, zi in ((slice(0, h), er + or_, ei + oi), (slice(h, 2 * h), er - or_, oi - ei)):
        kr, ki = kre_ref[rows, :], kim_ref[rows, :]
        yre_ref[rows, :] = (zr * kr - zi * ki).astype(out)
        yim_ref[rows, :] = (zr * ki + zi * kr).astype(out)
    r = 16
    first = _first_row((r, er.shape[1]))
    e, o, p, q = er[:r], or_[:r], ei[:r], oi[:r]
    krt, krb, kit, kib = kre_ref[0:r, :], kre_ref[h:h + r, :], kim_ref[0:r, :], kim_ref[h:h + r, :]
    re4 = p * kit - q * kib
    im4 = p * kib + q * kit
    yre_ref[0:r, :] = jnp.where(first, (e + o) * krt, (e + o) * krt - (p + q) * kit).astype(out)
    yim_ref[0:r, :] = jnp.where(first, re4 + im4, (e + o) * kit + (p + q) * krt).astype(out)
    yre_ref[h:h + r, :] = jnp.where(first, (e - o) * krb, (e - o) * krb - (q - p) * kib).astype(out)
    yim_ref[h:h + r, :] = jnp.where(first, im4 - re4, (e - o) * kib + (q - p) * krb).astype(out)


def _is_natural(operand):
    return isinstance(operand[1], int)


def _dft_fwd(fwd, z, conv, kre, kim, order, tn=512):
    b = z[0].shape[0]
    c = HYENA_WIDTH
    nc = c // tn
    out = jax.ShapeDtypeStruct((b, SEQ, c), BF16)
    z_specs, z_ops = _seq_specs(z, 1, tn, lambda j, i: ((i,), j))
    in_specs = [pl.BlockSpec(fwd.shape, lambda j, i: (0, 0, 0), pipeline_mode=pl.Buffered(1))]
    in_specs += z_specs
    operands = [fwd] + z_ops
    if conv is not None:
        cw, cb, col = conv
        in_specs += [pl.BlockSpec((cw.shape[0], tn), lambda j, i: (0, col // tn + j)),
                     pl.BlockSpec((1, tn), lambda j, i: (0, col // tn + j))]
        operands += [cw, cb]
    in_specs += [pl.BlockSpec((SEQ, tn), lambda j, i: (0, order * nc + j),
                              pipeline_mode=pl.Buffered(1))] * 2
    operands += [kre, kim]
    return pl.pallas_call(
        functools.partial(_dft_fwd_body, nz=len(z_ops), natural=_is_natural(z),
                          conv=conv is not None),
        grid=(nc, b),
        in_specs=in_specs,
        out_specs=[pl.BlockSpec((None, SEQ, tn), lambda j, i: (i, 0, j))] * 2,
        out_shape=[out, out],
        compiler_params=_params("parallel", "arbitrary"),
        name="dft_fwd",
    )(*operands)


def _dft_inv_body(mats_ref, yre_ref, yim_ref, *refs, nz, z_natural, conv_z, ng, final):
    ze, zo = _load_seq(refs[:nz], z_natural)
    refs = refs[nz:]
    if conv_z:
        ze, zo = _shortconv_eo(ze, zo, *refs[:2])
        refs = refs[2:]
    ge, go = _shortconv_eo(*_load_seq(refs[:ng], True), *refs[ng:ng + 2])
    b_ref, gn_ref, *outs = refs[ng + 2:]
    h = DFT_HALF
    yrt, yrb = yre_ref[0:h, :].astype(F32), yre_ref[h:, :].astype(F32)
    yit, yib = yim_ref[0:h, :].astype(F32), yim_ref[h:, :].astype(F32)
    ye = _dot(mats_ref[0], (yrt + yrb).astype(BF16)) + _dot(mats_ref[1], (yit - yib).astype(BF16))
    yo = _dot(mats_ref[2], (yrt - yrb).astype(BF16)) + _dot(mats_ref[3], (yit + yib).astype(BF16))
    bias = b_ref[...]
    halves = (ge * (ye + bias * ze), go * (yo + bias * zo))
    if not final:
        for o_ref, zn in zip(outs, halves):
            o_ref[...] = zn
        return
    o_ref, stage_ref = outs
    for gi in range(halves[0].shape[1] // HYENA_GROUP):
        sl = slice(gi * HYENA_GROUP, (gi + 1) * HYENA_GROUP)
        for parity, zn in enumerate(halves):
            blk = zn[:, sl]
            blk = blk * lax.rsqrt(jnp.mean(blk * blk, axis=-1, keepdims=True) + RMS_EPS)
            stage_ref[pl.ds(parity, h, stride=2), :] = blk * gn_ref[:, sl]
        o_ref[:, sl] = stage_ref[...].astype(o_ref.dtype)


def _dft_inv(inv, yre, yim, z, conv_z, gate, conv_g, bias, gnorm, final, tn=512):
    b = yre.shape[0]
    c = HYENA_WIDTH
    index = lambda j, i: ((i,), j)
    taps = lambda cw, col: [pl.BlockSpec((cw.shape[0], tn), lambda j, i: (0, col // tn + j)),
                            pl.BlockSpec((1, tn), lambda j, i: (0, col // tn + j))]
    z_specs, z_ops = _seq_specs(z, 1, tn, index)
    g_specs, g_ops = _seq_specs(gate, 1, tn, index)
    in_specs = [pl.BlockSpec(inv.shape, lambda j, i: (0, 0, 0), pipeline_mode=pl.Buffered(1)),
                pl.BlockSpec((None, SEQ, tn), lambda j, i: (i, 0, j)),
                pl.BlockSpec((None, SEQ, tn), lambda j, i: (i, 0, j))] + z_specs
    operands = [inv, yre, yim] + z_ops
    if conv_z is not None:
        in_specs += taps(conv_z[0], conv_z[2])
        operands += list(conv_z[:2])
    in_specs += g_specs + taps(conv_g[0], conv_g[2])
    in_specs += [pl.BlockSpec((1, tn), lambda j, i: (0, j))] * 2
    operands += g_ops + [conv_g[0], conv_g[1], bias.reshape(1, c), gnorm.reshape(1, c)]
    if final:
        out_specs = pl.BlockSpec((None, SEQ, tn), lambda j, i: (i, 0, j))
        out_shape = jax.ShapeDtypeStruct((b, SEQ, c), BF16)
        scratch = [pltpu.VMEM((SEQ, HYENA_GROUP), F32)]
    else:
        out_specs = [pl.BlockSpec((None, DFT_HALF, tn), lambda j, i: (i, 0, j))] * 2
        out_shape = [jax.ShapeDtypeStruct((b, DFT_HALF, c), F32)] * 2
        scratch = []
    return pl.pallas_call(
        functools.partial(_dft_inv_body, nz=len(z_ops), z_natural=_is_natural(z),
                          conv_z=conv_z is not None, ng=len(g_ops), final=final),
        grid=(c // tn, b),
        in_specs=in_specs,
        out_specs=out_specs,
        out_shape=out_shape,
        scratch_shapes=scratch,
        compiler_params=_params("parallel", "arbitrary"),
        name="dft_inv",
    )(*operands)


def _hyena(proj, dft, conv_w, conv_b, w1, b1, w2, b2, freq, w3, filt_bias, g_out):
    fwd, inv = dft
    c = HYENA_WIDTH
    raw = IN_COLS - (HYENA_ORDER + 1) * c
    conv_b = conv_b.reshape(1, -1)
    filt = _hyena_filters(w1, b1, w2, b2, freq, w3)
    kre, kim = _kspec(fwd, filt)
    z, conv_z = (proj, raw), (conv_w, conv_b, 0)
    for order in range(HYENA_ORDER):
        final = order == HYENA_ORDER - 1
        yre, yim = _dft_fwd(fwd, z, conv_z, kre, kim, order)
        z = _dft_inv(inv, yre, yim, z, conv_z, (proj, raw + (order + 1) * c),
                     (conv_w, conv_b, (order + 1) * c), filt_bias[order], g_out, final)
        conv_z = None
    return z


def _rotary_tables():
    half = HEAD_DIM // 2
    pos = jnp.arange(SEQ, dtype=F32)
    inv = ROPE_THETA ** (-jnp.arange(half, dtype=F32) / half)
    ang = pos[:, None] * inv[None, :]
    cos = jnp.cos(ang)
    sin = jnp.sin(ang)
    return jnp.concatenate([cos, cos], axis=-1), jnp.concatenate([-sin, sin], axis=-1)


def _mixer(proj, dft, rot, q_norm, k_norm, conv_w, conv_b, w1, b1, w2, b2, freq, w3,
           filt_bias, attn_out_norm, hyena_out_norm):
    attn = _attention(proj, q_norm, k_norm, rot[0], rot[1], attn_out_norm)
    hy = _hyena(proj, dft, conv_w, conv_b, w1, b1, w2, b2, freq, w3, filt_bias, hyena_out_norm)
    return attn, hy


def _ffn(x, g, w_gate, w_up, w_down, idx):
    a, wd = _matmul(_swiglu_cast_body, [_rmsnorm(x, g)], [(w_gate, idx), (w_up, idx)], None, BF16,
                    2048, 256, "ffn_up", cast_rows=(w_down, idx))
    return _matmul(functools.partial(_mm_res_body, scale=0.5), [a], [(wd, ())], x, F32,
                   1024, 256, "ffn_down")


def kernel(x, ffn_norm, ffn_w_gate, ffn_w_up, ffn_w_down, mix_norm, w_in, q_norm, k_norm, conv_w, conv_b, filt_w1, filt_b1, filt_w2, filt_b2, filt_freq, filt_w3, filt_bias, attn_out_norm, hyena_out_norm, w_out):
    b, s, d = x.shape
    x = x.reshape(b * s, d)
    dft = _dft_matrices()
    rot = _rotary_tables()
    for l in range(DEPTH):
        x = _ffn(x, ffn_norm[l, 0], ffn_w_gate, ffn_w_up, ffn_w_down, (l, 0))
        proj = _matmul(_mm_body, [_rmsnorm(x, mix_norm[l])], [(w_in, (l,))], None, F32,
                       2048, 256, "in_proj")
        attn, hy = _mixer(proj.reshape(b, s, IN_COLS), dft, rot, q_norm[l], k_norm[l], conv_w[l],
                          conv_b[l], filt_w1[l], filt_b1[l], filt_w2[l], filt_b2[l], filt_freq[l],
                          filt_w3[l], filt_bias[l], attn_out_norm[l], hyena_out_norm[l])
        x = _matmul(_mm2_res_body, [attn.reshape(b * s, -1), hy.reshape(b * s, -1)],
                    [(w_out, (l,))], x, F32, 2048, 256, "out_proj")
        x = _ffn(x, ffn_norm[l, 1], ffn_w_gate, ffn_w_up, ffn_w_down, (l, 1))
    return x.reshape(b, s, d)
```

```python
import functools
import math

import numpy as np
import jax
import jax.numpy as jnp
from jax import lax
from jax.experimental import pallas as pl
from jax.experimental.pallas import tpu as pltpu

D_MODEL = 4096
BATCH = 4
SEQ = 2048
DEPTH = 2
ATTN_WIDTH = D_MODEL // 2
HYENA_WIDTH = D_MODEL - ATTN_WIDTH
HEAD_DIM = 128
N_HEADS = ATTN_WIDTH // HEAD_DIM
DILATED_CONFIGS = ((128, 1), (512, 4), (2048, 16))
ROPE_THETA = 10000.0
HYENA_ORDER = 2
HYENA_GROUP = 128
FILTER_EMB_DIM = 33
FILTER_HIDDEN = 64
DECAY_FAST = 0.3
DECAY_SLOW = 1.5
DECAY_TARGET = 1e-2
FFN_HIDDEN = 256 * (-(-(8 * D_MODEL) // (3 * 256)))
IN_COLS = 3 * ATTN_WIDTH + (HYENA_ORDER + 1) * HYENA_WIDTH
RMS_EPS = 1e-6
NEG_INF = -1e30

F32 = jnp.float32
BF16 = jnp.bfloat16

V7X_VMEM_LIMIT_BYTES = 56 * 1024 * 1024
LANES = 128
ATTN_TQ = 256
ATTN_REACH = max((w // (2 * d)) * d for w, d in DILATED_CONFIGS)
DFT_N = 2 * SEQ
DFT_HALF = SEQ // 2
DFT_SPLIT = 32


def _params(*sem):
    return pltpu.CompilerParams(dimension_semantics=sem, vmem_limit_bytes=V7X_VMEM_LIMIT_BYTES)


def _dot(a, b):
    return jnp.dot(a, b, preferred_element_type=F32)


def _lane_group_sum(v):
    out = v[:, 0:LANES]
    for k in range(1, v.shape[1] // LANES):
        out = out + v[:, k * LANES:(k + 1) * LANES]
    return out


def _row_factor(lane_sums, width):
    ssq = jnp.sum(lane_sums, axis=-1, keepdims=True)
    return jnp.broadcast_to(lax.rsqrt(ssq / width + RMS_EPS), lane_sums.shape)


def _row_scale(rf_ref, width):
    return jnp.concatenate([rf_ref[...]] * (width // LANES), axis=1)


def _norm_prep_body(x_ref, g_ref, xg_ref, rf_ref):
    x = x_ref[...]
    xg_ref[...] = (x * g_ref[...]).astype(xg_ref.dtype)
    rf_ref[...] = _row_factor(_lane_group_sum(x * x), x.shape[1])


def _norm_prep(x, g, tm=512):
    m, d = x.shape
    return pl.pallas_call(
        _norm_prep_body,
        grid=(m // tm,),
        in_specs=[pl.BlockSpec((tm, d), lambda i: (i, 0)),
                  pl.BlockSpec((1, d), lambda i: (0, 0))],
        out_specs=[pl.BlockSpec((tm, d), lambda i: (i, 0)),
                   pl.BlockSpec((tm, LANES), lambda i: (i, 0))],
        out_shape=[jax.ShapeDtypeStruct((m, d), BF16), jax.ShapeDtypeStruct((m, LANES), F32)],
        compiler_params=_params("parallel"),
        name="norm_prep",
    )(x, g.reshape(1, d))


def _swiglu_cast_body(a_ref, wg_ref, wu_ref, rf_ref, wd_ref, o_ref, wdb_ref):
    a = a_ref[...]
    r = _row_scale(rf_ref, o_ref.shape[1])
    g = r * _dot(a, wg_ref[...].astype(BF16))
    u = r * _dot(a, wu_ref[...].astype(BF16))
    o_ref[...] = (g * jax.nn.sigmoid(g) * u).astype(o_ref.dtype)

    @pl.when(pl.program_id(0) == 0)
    def _():
        wdb_ref[...] = wd_ref[...].astype(wdb_ref.dtype)


def _mm_body(a_ref, w_ref, rf_ref, o_ref):
    r = _row_scale(rf_ref, o_ref.shape[1])
    o_ref[...] = (r * _dot(a_ref[...], w_ref[...].astype(BF16))).astype(o_ref.dtype)


def _mm_res_body(*refs, na, scale, norm_width):
    a_refs, (w_ref, r_ref), refs = refs[:na], refs[na:na + 2], refs[na + 2:]
    acc, k0 = None, 0
    for a_ref in a_refs:
        k1 = k0 + a_ref.shape[1]
        part = _dot(a_ref[...], w_ref[k0:k1, :].astype(BF16))
        acc, k0 = part if acc is None else acc + part, k1
    x = r_ref[...] + scale * acc
    if not norm_width:
        o_ref, = refs
        o_ref[...] = x
        return
    g_ref, o_ref, xg_ref, rf_ref = refs
    o_ref[...] = x
    xg_ref[...] = (x * g_ref[...]).astype(xg_ref.dtype)
    part = _lane_group_sum(x * x)
    j = pl.program_id(1)

    @pl.when(j == 0)
    def _():
        rf_ref[...] = part

    @pl.when(j > 0)
    def _():
        rf_ref[...] += part

    @pl.when(j == pl.num_programs(1) - 1)
    def _():
        rf_ref[...] = _row_factor(rf_ref[...], norm_width)


def _matmul(body, a_list, weights, out_dtype, tm, tn, name, *, row_factor=None, res=None,
            norm_next=None, cast_rows=None):
    m = a_list[0].shape[0]
    n = weights[0][0].shape[-1]
    k = sum(a.shape[1] for a in a_list)
    nj = n // tn
    tile = pl.BlockSpec((tm, tn), lambda i, j: (i, j))
    rows = pl.BlockSpec((tm, LANES), lambda i, j: (i, 0))
    in_specs = [pl.BlockSpec((tm, a.shape[1]), lambda i, j: (i, 0), pipeline_mode=pl.Buffered(1))
                for a in a_list]
    for w, prefix in weights:
        in_specs.append(pl.BlockSpec((None,) * len(prefix) + (k, tn),
                                     lambda i, j, prefix=prefix: prefix + (0, j)))
    operands = list(a_list) + [w for w, _ in weights]
    out_specs = [tile]
    out_shape = [jax.ShapeDtypeStruct((m, n), out_dtype)]
    if row_factor is not None:
        in_specs.append(rows)
        operands.append(row_factor)
    if res is not None:
        in_specs.append(tile)
        operands.append(res)
    if norm_next is not None:
        in_specs.append(pl.BlockSpec((1, tn), lambda i, j: (0, j)))
        operands.append(norm_next.reshape(1, n))
        out_specs += [tile, rows]
        out_shape += [jax.ShapeDtypeStruct((m, n), BF16), jax.ShapeDtypeStruct((m, LANES), F32)]
    if cast_rows is not None:
        w, prefix = cast_rows
        cols = w.shape[-1]
        row_block = lambda i, j: jnp.where(i == 0, j, nj - 1)
        in_specs.append(pl.BlockSpec((None,) * len(prefix) + (tn, cols),
                                     lambda i, j: prefix + (row_block(i, j), 0)))
        operands.append(w)
        out_specs.append(pl.BlockSpec((tn, cols), lambda i, j: (row_block(i, j), 0)))
        out_shape.append(jax.ShapeDtypeStruct((n, cols), BF16))
    return pl.pallas_call(
        body,
        grid=(m // tm, nj),
        in_specs=in_specs,
        out_specs=out_specs,
        out_shape=out_shape,
        compiler_params=_params("arbitrary", "arbitrary"),
        name=name,
    )(*operands)


def _attn_bias_table():
    il = np.arange(ATTN_TQ)[:, None]
    c = np.arange(ATTN_TQ + 2 * ATTN_REACH)[None, :]
    d = il + ATTN_REACH - c
    mult = np.zeros(d.shape, np.int64)
    for window, dil in DILATED_CONFIGS:
        half = window // (2 * dil)
        mult += ((d % dil) == 0) & (np.abs(d) <= half * dil)
    return np.where(mult > 0, np.log(np.maximum(mult, 1)), NEG_INF).astype(np.float32)


def _attn_body(q_ref, k_ref, v_ref, gq_ref, gk_ref, cos_ref, sin_ref, bias_ref, go_ref,
               o_ref, qs_ref, ks_ref, vs_ref):
    seq = q_ref.shape[1]
    cos = cos_ref[...]
    sin = sin_ref[...]

    def norm_rot(x, g):
        y = x * lax.rsqrt(jnp.mean(x * x, axis=-1, keepdims=True) + RMS_EPS) * g
        return y * cos + pltpu.roll(y, HEAD_DIM // 2, axis=1) * sin

    qs_ref[...] = (norm_rot(q_ref[0], gq_ref[...]) * (1.0 / math.sqrt(HEAD_DIM))).astype(BF16)
    ks_ref[...] = norm_rot(k_ref[0], gk_ref[...]).astype(BF16)
    vs_ref[...] = v_ref[0].astype(BF16)

    for t in range(seq // ATTN_TQ):
        q0 = t * ATTN_TQ
        lo = max(0, q0 - ATTN_REACH)
        hi = min(seq, q0 + ATTN_TQ + ATTN_REACH)
        c0 = lo - (q0 - ATTN_REACH)
        s = lax.dot_general(qs_ref[q0:q0 + ATTN_TQ, :], ks_ref[lo:hi, :],
                            (((1,), (1,)), ((), ())), preferred_element_type=F32)
        s = s + bias_ref[:, c0:c0 + (hi - lo)]
        m = jnp.max(s, axis=-1, keepdims=True)
        p = jnp.exp(s - m)
        den = jnp.sum(p, axis=-1, keepdims=True)
        o = _dot(p.astype(BF16), vs_ref[lo:hi, :]) / den
        o = o * lax.rsqrt(jnp.mean(o * o, axis=-1, keepdims=True) + RMS_EPS) * go_ref[...]
        o_ref[0, q0:q0 + ATTN_TQ, :] = o.astype(o_ref.dtype)


def _attention(proj, gq, gk, cos, sin_signed, g_out):
    b, s, _ = proj.shape
    hd = HEAD_DIM
    bias = jnp.asarray(_attn_bias_table())
    head = lambda off: pl.BlockSpec((1, s, hd), lambda i, h: (i, 0, off + h))
    const2 = lambda shape: pl.BlockSpec(shape, lambda i, h: (0, 0))
    return pl.pallas_call(
        _attn_body,
        grid=(b, N_HEADS),
        in_specs=[head(0), head(N_HEADS), head(2 * N_HEADS),
                  const2((1, hd)), const2((1, hd)),
                  const2((s, hd)), const2((s, hd)),
                  const2(bias.shape),
                  pl.BlockSpec((1, hd), lambda i, h: (0, h))],
        out_specs=pl.BlockSpec((1, s, hd), lambda i, h: (i, 0, h)),
        out_shape=jax.ShapeDtypeStruct((b, s, ATTN_WIDTH), BF16),
        scratch_shapes=[pltpu.VMEM((s, hd), BF16)] * 3,
        compiler_params=_params("parallel", "arbitrary"),
        name="attention",
    )(proj, proj, proj, gq.reshape(1, hd), gk.reshape(1, hd), cos, sin_signed, bias,
      g_out.reshape(1, ATTN_WIDTH))


def _shortconv_eo(ue, uo, w_ref, b_ref):
    half, edge = ue.shape[0], 8
    w0, w1, w2, b = w_ref[0:1, :], w_ref[1:2, :], w_ref[2:3, :], b_ref[...]
    taps = lambda prev, cur, nxt: w0 * prev + w1 * cur + w2 * nxt + b
    uo_prev = pltpu.roll(uo, 1, axis=0)
    ue_next = pltpu.roll(ue, half - 1, axis=0)
    row = lax.broadcasted_iota(jnp.int32, (edge, ue.shape[1]), 0)
    lo, hi = slice(0, edge), slice(half - edge, half)
    even_top = taps(jnp.where(row == 0, 0.0, uo_prev[lo]), ue[lo], uo[lo])
    odd_bot = taps(ue[hi], uo[hi], jnp.where(row == edge - 1, 0.0, ue_next[hi]))
    even = jnp.concatenate([even_top, taps(uo_prev, ue, uo)[edge:]], axis=0)
    odd = jnp.concatenate([taps(ue, uo, ue_next)[:half - edge], odd_bot], axis=0)
    return even, odd


def _filt_hidden_body(emb_ref, w1_ref, b1_ref, w2_ref, b2_ref, fr_ref, o_ref):
    hp = lax.Precision.HIGHEST
    h = jnp.dot(emb_ref[...], w1_ref[...], precision=hp, preferred_element_type=F32)
    h = jnp.sin(fr_ref[0:1, :] * (h + b1_ref[...]))
    h = jnp.dot(h, w2_ref[...], precision=hp, preferred_element_type=F32)
    o_ref[...] = jnp.sin(fr_ref[1:2, :] * (h + b2_ref[...]))


def _filt_out_body(h_ref, w3_ref, t_ref, delta_ref, o_ref):
    h = jnp.dot(h_ref[...], w3_ref[...], precision=lax.Precision.HIGHEST,
                preferred_element_type=F32)
    o_ref[...] = h * jnp.exp(-t_ref[...] * delta_ref[...])


def _pad_to(a, shape):
    return jnp.pad(a, [(0, t - s) for s, t in zip(a.shape, shape)])


def _hyena_filters(w1, b1, w2, b2, freq, w3, tn=512):
    seq = SEQ
    t = jnp.linspace(0.0, 1.0, seq, dtype=F32)[:, None]
    bands = (FILTER_EMB_DIM - 1) // 2
    f = jnp.linspace(1e-4, bands - 1, bands, dtype=F32)[None, :]
    wpos = (2.0 * math.pi) * jnp.arange(seq, dtype=F32)[:, None] / seq
    emb = jnp.concatenate([t, jnp.cos(f * wpos), -jnp.sin(f * wpos)], axis=-1)
    deltas = jnp.abs(jnp.linspace(math.log(DECAY_FAST) / DECAY_TARGET,
                                  math.log(DECAY_SLOW) / DECAY_TARGET, HYENA_WIDTH, dtype=F32))
    ncol = w3.shape[1]
    deltas = jnp.tile(deltas, ncol // HYENA_WIDTH).reshape(1, ncol)
    hid = LANES
    hidden = pl.pallas_call(
        _filt_hidden_body,
        out_shape=jax.ShapeDtypeStruct((seq, hid), F32),
        name="filt_hidden",
    )(_pad_to(emb, (seq, hid)), _pad_to(w1, (hid, hid)), _pad_to(b1[None], (1, hid)),
      _pad_to(w2, (hid, hid)), _pad_to(b2[None], (1, hid)), _pad_to(freq, (2, hid)))
    return pl.pallas_call(
        _filt_out_body,
        grid=(ncol // tn,),
        in_specs=[pl.BlockSpec((seq, hid), lambda j: (0, 0)),
                  pl.BlockSpec((hid, tn), lambda j: (0, j)),
                  pl.BlockSpec((seq, 1), lambda j: (0, 0)),
                  pl.BlockSpec((1, tn), lambda j: (0, j))],
        out_specs=pl.BlockSpec((seq, tn), lambda j: (0, j)),
        out_shape=jax.ShapeDtypeStruct((seq, ncol), F32),
        compiler_params=_params("parallel"),
        name="filt_out",
    )(hidden, _pad_to(w3, (hid, ncol)), t, deltas)


def _dft_matrices():
    h = DFT_HALF
    def table(t, n):
        ang = ((t[:, None] * n[None, :]) & (DFT_N - 1)).astype(F32) * (2.0 * math.pi / DFT_N)
        return jnp.cos(ang), jnp.sin(ang)
    def cos_sin(n):
        c_hi, s_hi = (x[:, None, :] for x in table(jnp.arange(0, h, DFT_SPLIT, dtype=jnp.int32), n))
        c_lo, s_lo = (x[None, :, :] for x in table(jnp.arange(DFT_SPLIT, dtype=jnp.int32), n))
        return ((c_hi * c_lo - s_hi * s_lo).reshape(h, -1),
                (s_hi * c_lo + c_hi * s_lo).reshape(h, -1))
    even = 2 * jnp.arange(h, dtype=jnp.int32)
    (ce, se), (co, so) = cos_sin(even), cos_sin(even + 1)
    se, so = -se, -so
    alt = jnp.where(jnp.arange(h) % 2 == 0, 1.0, -1.0).astype(F32)
    first = jnp.arange(h) == 0
    fwd = [ce, co, jnp.where(first[:, None], alt[None, :], se),
           jnp.where(first[:, None], -alt[None, :], so)]
    wgt = jnp.where(first, 1.0 / DFT_N, 2.0 / DFT_N)[None, :]
    inv = [ce.T * wgt, jnp.where(first[None, :], alt[:, None] / DFT_N, se.T * wgt),
           co.T * wgt, jnp.where(first[None, :], -alt[:, None] / DFT_N, so.T * wgt)]
    return jnp.stack(fwd).astype(BF16), jnp.stack(inv).astype(BF16)


def _rdft(mats_ref, ze, zo):
    ze = ze.astype(BF16)
    zo = zo.astype(BF16)
    return (_dot(mats_ref[0], ze), _dot(mats_ref[1], zo), _dot(mats_ref[2], ze),
            _dot(mats_ref[3], zo))


def _first_row(shape):
    return lax.broadcasted_iota(jnp.int32, shape, 0) == 0


def _seq_specs(operand, lead, tn, index):
    def spec(rows, width, offset):
        def index_map(*g):
            pre, j = index(*g)
            return pre + (0, offset + j * (tn // width))
        return pl.BlockSpec((None,) * lead + (rows, width), index_map)
    if not _is_natural(operand):
        return [spec(DFT_HALF, tn, 0)] * 2, list(operand)
    array, col = operand
    groups = tn // LANES
    return [spec(SEQ, LANES, col // LANES + g) for g in range(groups)], [array] * groups


def _load_seq(refs, natural):
    if not natural:
        return refs[0][...], refs[1][...]
    parity = lambda p: jnp.concatenate([r[pl.ds(p, DFT_HALF, stride=2), :] for r in refs], axis=1)
    return parity(0), parity(1)


def _kspec_body(mats_ref, *refs):
    h = DFT_HALF
    *h_refs, hb0_ref, kre_ref, kim_ref = refs
    n = len(h_refs) // 2
    fer, for_, fei, foi = _rdft(mats_ref, *_load_seq(h_refs[:n], True))
    ber, bor, bei, boi = _rdft(mats_ref, *_load_seq(h_refs[n:], True))
    hb0 = hb0_ref[0:1, :]
    kre_ref[0:h, :] = (fer + for_) + (ber + bor) - hb0
    kre_ref[h:, :] = (fer - for_) + (ber - bor) - hb0
    kim_ref[0:h, :] = (fei + foi) - (bei + boi)
    kim_ref[h:, :] = (foi - fei) - (boi - bei)
    r = 8
    first = _first_row((r, fer.shape[1]))
    kim_ref[0:r, :] = jnp.where(first, fei[:r] + bei[:r] - hb0, (fei + foi - bei - boi)[:r])
    kim_ref[h:h + r, :] = jnp.where(first, foi[:r] - boi[:r], (foi - fei - boi + bei)[:r])


def _kspec(fwd, filt, tn=256):
    c = HYENA_WIDTH
    nc = c // tn
    out = jax.ShapeDtypeStruct((SEQ, HYENA_ORDER * c), F32)
    index = lambda o, j: ((), o * 2 * nc + j)
    hf_specs, hf_ops = _seq_specs((filt, 0), 0, tn, index)
    hb_specs, hb_ops = _seq_specs((filt, c), 0, tn, index)
    return pl.pallas_call(
        _kspec_body,
        grid=(HYENA_ORDER, nc),
        in_specs=[pl.BlockSpec(fwd.shape, lambda o, j: (0, 0, 0), pipeline_mode=pl.Buffered(1))]
        + hf_specs + hb_specs + [pl.BlockSpec((8, tn), lambda o, j: (0, o * 2 * nc + nc + j))],
        out_specs=[pl.BlockSpec((SEQ, tn), lambda o, j: (0, o * nc + j))] * 2,
        out_shape=[out, out],
        compiler_params=_params("parallel", "arbitrary"),
        name="kspec",
    )(fwd, *hf_ops, *hb_ops, filt)


def _dft_fwd_body(mats_ref, *refs, nz, natural, conv):
    ze, zo = _load_seq(refs[:nz], natural)
    if conv:
        ze, zo = _shortconv_eo(ze, zo, *refs[nz:nz + 2])
    kre_ref, kim_ref, yre_ref, yim_ref = refs[-4:]
    h = DFT_HALF
    er, or_, ei, oi = _rdft(mats_ref, ze, zo)
    out = yre_ref.dtype
    for rows, zr, zi in ((slice(0, h), er + or_, ei + oi), (slice(h, 2 * h), er - or_, oi - ei)):
        kr, ki = kre_ref[rows, :], kim_ref[rows, :]
        yre_ref[rows, :] = (zr * kr - zi * ki).astype(out)
        yim_ref[rows, :] = (zr * ki + zi * kr).astype(out)
    r = 16
    first = _first_row((r, er.shape[1]))
    e, o, p, q = er[:r], or_[:r], ei[:r], oi[:r]
    krt, krb, kit, kib = kre_ref[0:r, :], kre_ref[h:h + r, :], kim_ref[0:r, :], kim_ref[h:h + r, :]
    re4 = p * kit - q * kib
    im4 = p * kib + q * kit
    yre_ref[0:r, :] = jnp.where(first, (e + o) * krt, (e + o) * krt - (p + q) * kit).astype(out)
    yim_ref[0:r, :] = jnp.where(first, re4 + im4, (e + o) * kit + (p + q) * krt).astype(out)
    yre_ref[h:h + r, :] = jnp.where(first, (e - o) * krb, (e - o) * krb - (q - p) * kib).astype(out)
    yim_ref[h:h + r, :] = jnp.where(first, im4 - re4, (e - o) * kib + (q - p) * krb).astype(out)


def _is_natural(operand):
    return isinstance(operand[1], int)


def _dft_fwd(fwd, z, conv, kre, kim, order, tn=512):
    b = z[0].shape[0]
    c = HYENA_WIDTH
    nc = c // tn
    out = jax.ShapeDtypeStruct((b, SEQ, c), BF16)
    z_specs, z_ops = _seq_specs(z, 1, tn, lambda j, i: ((i,), j))
    in_specs = [pl.BlockSpec(fwd.shape, lambda j, i: (0, 0, 0), pipeline_mode=pl.Buffered(1))]
    in_specs += z_specs
    operands = [fwd] + z_ops
    if conv is not None:
        cw, cb, col = conv
        in_specs += [pl.BlockSpec((cw.shape[0], tn), lambda j, i: (0, col // tn + j)),
                     pl.BlockSpec((1, tn), lambda j, i: (0, col // tn + j))]
        operands += [cw, cb]
    in_specs += [pl.BlockSpec((SEQ, tn), lambda j, i: (0, order * nc + j),
                              pipeline_mode=pl.Buffered(1))] * 2
    operands += [kre, kim]
    return pl.pallas_call(
        functools.partial(_dft_fwd_body, nz=len(z_ops), natural=_is_natural(z),
                          conv=conv is not None),
        grid=(nc, b),
        in_specs=in_specs,
        out_specs=[pl.BlockSpec((None, SEQ, tn), lambda j, i: (i, 0, j))] * 2,
        out_shape=[out, out],
        compiler_params=_params("parallel", "arbitrary"),
        name="dft_fwd",
    )(*operands)


def _dft_inv_body(mats_ref, yre_ref, yim_ref, *refs, nz, z_natural, conv_z, ng, final):
    ze, zo = _load_seq(refs[:nz], z_natural)
    refs = refs[nz:]
    if conv_z:
        ze, zo = _shortconv_eo(ze, zo, *refs[:2])
        refs = refs[2:]
    ge, go = _shortconv_eo(*_load_seq(refs[:ng], True), *refs[ng:ng + 2])
    b_ref, gn_ref, *outs = refs[ng + 2:]
    h = DFT_HALF
    yrt, yrb = yre_ref[0:h, :].astype(F32), yre_ref[h:, :].astype(F32)
    yit, yib = yim_ref[0:h, :].astype(F32), yim_ref[h:, :].astype(F32)
    ye = _dot(mats_ref[0], (yrt + yrb).astype(BF16)) + _dot(mats_ref[1], (yit - yib).astype(BF16))
    yo = _dot(mats_ref[2], (yrt - yrb).astype(BF16)) + _dot(mats_ref[3], (yit + yib).astype(BF16))
    bias = b_ref[...]
    halves = (ge * (ye + bias * ze), go * (yo + bias * zo))
    if not final:
        for o_ref, zn in zip(outs, halves):
            o_ref[...] = zn
        return
    o_ref, stage_ref = outs
    for gi in range(halves[0].shape[1] // HYENA_GROUP):
        sl = slice(gi * HYENA_GROUP, (gi + 1) * HYENA_GROUP)
        for parity, zn in enumerate(halves):
            blk = zn[:, sl]
            blk = blk * lax.rsqrt(jnp.mean(blk * blk, axis=-1, keepdims=True) + RMS_EPS)
            stage_ref[pl.ds(parity, h, stride=2), :] = blk * gn_ref[:, sl]
        o_ref[:, sl] = stage_ref[...].astype(o_ref.dtype)


def _dft_inv(inv, yre, yim, z, conv_z, gate, conv_g, bias, gnorm, final, tn=512):
    b = yre.shape[0]
    c = HYENA_WIDTH
    index = lambda j, i: ((i,), j)
    taps = lambda cw, col: [pl.BlockSpec((cw.shape[0], tn), lambda j, i: (0, col // tn + j)),
                            pl.BlockSpec((1, tn), lambda j, i: (0, col // tn + j))]
    z_specs, z_ops = _seq_specs(z, 1, tn, index)
    g_specs, g_ops = _seq_specs(gate, 1, tn, index)
    in_specs = [pl.BlockSpec(inv.shape, lambda j, i: (0, 0, 0), pipeline_mode=pl.Buffered(1)),
                pl.BlockSpec((None, SEQ, tn), lambda j, i: (i, 0, j)),
                pl.BlockSpec((None, SEQ, tn), lambda j, i: (i, 0, j))] + z_specs
    operands = [inv, yre, yim] + z_ops
    if conv_z is not None:
        in_specs += taps(conv_z[0], conv_z[2])
        operands += list(conv_z[:2])
    in_specs += g_specs + taps(conv_g[0], conv_g[2])
    in_specs += [pl.BlockSpec((1, tn), lambda j, i: (0, j))] * 2
    operands += g_ops + [conv_g[0], conv_g[1], bias.reshape(1, c), gnorm.reshape(1, c)]
    if final:
        out_specs = pl.BlockSpec((None, SEQ, tn), lambda j, i: (i, 0, j))
        out_shape = jax.ShapeDtypeStruct((b, SEQ, c), BF16)
        scratch = [pltpu.VMEM((SEQ, HYENA_GROUP), F32)]
    else:
        out_specs = [pl.BlockSpec((None, DFT_HALF, tn), lambda j, i: (i, 0, j))] * 2
        out_shape = [jax.ShapeDtypeStruct((b, DFT_HALF, c), F32)] * 2
        scratch = []
    return pl.pallas_call(
        functools.partial(_dft_inv_body, nz=len(z_ops), z_natural=_is_natural(z),
                          conv_z=conv_z is not None, ng=len(g_ops), final=final),
        grid=(c // tn, b),
        in_specs=in_specs,
        out_specs=out_specs,
        out_shape=out_shape,
        scratch_shapes=scratch,
        compiler_params=_params("parallel", "arbitrary"),
        name="dft_inv",
    )(*operands)


def _hyena(proj, dft, conv_w, conv_b, w1, b1, w2, b2, freq, w3, filt_bias, g_out):
    fwd, inv = dft
    c = HYENA_WIDTH
    raw = IN_COLS - (HYENA_ORDER + 1) * c
    conv_b = conv_b.reshape(1, -1)
    filt = _hyena_filters(w1, b1, w2, b2, freq, w3)
    kre, kim = _kspec(fwd, filt)
    z, conv_z = (proj, raw), (conv_w, conv_b, 0)
    for order in range(HYENA_ORDER):
        final = order == HYENA_ORDER - 1
        yre, yim = _dft_fwd(fwd, z, conv_z, kre, kim, order)
        z = _dft_inv(inv, yre, yim, z, conv_z, (proj, raw + (order + 1) * c),
                     (conv_w, conv_b, (order + 1) * c), filt_bias[order], g_out, final)
        conv_z = None
    return z


def _rotary_tables():
    half = HEAD_DIM // 2
    pos = jnp.arange(SEQ, dtype=F32)
    inv = ROPE_THETA ** (-jnp.arange(half, dtype=F32) / half)
    ang = pos[:, None] * inv[None, :]
    cos = jnp.cos(ang)
    sin = jnp.sin(ang)
    return jnp.concatenate([cos, cos], axis=-1), jnp.concatenate([-sin, sin], axis=-1)


def _mixer(proj, dft, rot, q_norm, k_norm, conv_w, conv_b, w1, b1, w2, b2, freq, w3,
           filt_bias, attn_out_norm, hyena_out_norm):
    attn = _attention(proj, q_norm, k_norm, rot[0], rot[1], attn_out_norm)
    hy = _hyena(proj, dft, conv_w, conv_b, w1, b1, w2, b2, freq, w3, filt_bias, hyena_out_norm)
    return attn, hy


def _residual_matmul(a_list, weight, x, scale, norm_next, tm, name):
    width = weight[0].shape[-1]
    body = functools.partial(_mm_res_body, na=len(a_list), scale=scale,
                             norm_width=0 if norm_next is None else width)
    x, *normed = _matmul(body, a_list, [weight], F32, tm, 256, name, res=x, norm_next=norm_next)
    return x, normed


def _ffn(x, normed, w_gate, w_up, w_down, idx, norm_next):
    xg, rf = normed
    a, wd = _matmul(_swiglu_cast_body, [xg], [(w_gate, idx), (w_up, idx)], BF16, 2048, 256,
                    "ffn_up", row_factor=rf, cast_rows=(w_down, idx))
    return _residual_matmul([a], (wd, ()), x, 0.5, norm_next, 1024, "ffn_down")


def kernel(x, ffn_norm, ffn_w_gate, ffn_w_up, ffn_w_down, mix_norm, w_in, q_norm, k_norm, conv_w, conv_b, filt_w1, filt_b1, filt_w2, filt_b2, filt_freq, filt_w3, filt_bias, attn_out_norm, hyena_out_norm, w_out):
    b, s, d = x.shape
    x = x.reshape(b * s, d)
    dft = _dft_matrices()
    rot = _rotary_tables()
    normed = _norm_prep(x, ffn_norm[0, 0])
    for l in range(DEPTH):
        x, normed = _ffn(x, normed, ffn_w_gate, ffn_w_up, ffn_w_down, (l, 0), mix_norm[l])
        proj, = _matmul(_mm_body, [normed[0]], [(w_in, (l,))], F32, 2048, 256, "in_proj",
                        row_factor=normed[1])
        attn, hy = _mixer(proj.reshape(b, s, IN_COLS), dft, rot, q_norm[l], k_norm[l], conv_w[l],
                          conv_b[l], filt_w1[l], filt_b1[l], filt_w2[l], filt_b2[l], filt_freq[l],
                          filt_w3[l], filt_bias[l], attn_out_norm[l], hyena_out_norm[l])
        x, normed = _residual_matmul([attn.reshape(b * s, -1), hy.reshape(b * s, -1)],
                                     (w_out, (l,)), x, 1.0, ffn_norm[l, 1], 2048, "out_proj")
        last = l == DEPTH - 1
        x, normed = _ffn(x, normed, ffn_w_gate, ffn_w_up, ffn_w_down, (l, 1),
                         None if last else ffn_norm[l + 1, 0])
    return x.reshape(b, s, d)
```

```python
import functools
import math

import numpy as np
import jax
import jax.numpy as jnp
from jax import lax
from jax.experimental import pallas as pl
from jax.experimental.pallas import tpu as pltpu

D_MODEL = 4096
BATCH = 4
SEQ = 2048
DEPTH = 2
ATTN_WIDTH = D_MODEL // 2
HYENA_WIDTH = D_MODEL - ATTN_WIDTH
HEAD_DIM = 128
N_HEADS = ATTN_WIDTH // HEAD_DIM
DILATED_CONFIGS = ((128, 1), (512, 4), (2048, 16))
ROPE_THETA = 10000.0
HYENA_ORDER = 2
HYENA_GROUP = 128
FILTER_EMB_DIM = 33
FILTER_HIDDEN = 64
DECAY_FAST = 0.3
DECAY_SLOW = 1.5
DECAY_TARGET = 1e-2
FFN_HIDDEN = 256 * (-(-(8 * D_MODEL) // (3 * 256)))
IN_COLS = 3 * ATTN_WIDTH + (HYENA_ORDER + 1) * HYENA_WIDTH
RMS_EPS = 1e-6
NEG_INF = -1e30

F32 = jnp.float32
BF16 = jnp.bfloat16

V7X_VMEM_LIMIT_BYTES = 56 * 1024 * 1024
LANES = 128
ATTN_TQ = 256
ATTN_REACH = max((w // (2 * d)) * d for w, d in DILATED_CONFIGS)
DFT_N = 2 * SEQ
DFT_HALF = SEQ // 2
DFT_SPLIT = 32


def _params(*sem):
    return pltpu.CompilerParams(dimension_semantics=sem, vmem_limit_bytes=V7X_VMEM_LIMIT_BYTES)


def _dot(a, b):
    return jnp.dot(a, b, preferred_element_type=F32)


def _lane_group_sum(v):
    out = v[:, 0:LANES]
    for k in range(1, v.shape[1] // LANES):
        out = out + v[:, k * LANES:(k + 1) * LANES]
    return out


def _row_factor(lane_sums, width):
    ssq = jnp.sum(lane_sums, axis=-1, keepdims=True)
    return jnp.broadcast_to(lax.rsqrt(ssq / width + RMS_EPS), lane_sums.shape)


def _row_scale(rf_ref, width):
    return jnp.concatenate([rf_ref[...]] * (width // LANES), axis=1)


def _norm_prep_body(x_ref, g_ref, xg_ref, rf_ref):
    x = x_ref[...]
    xg_ref[...] = (x * g_ref[...]).astype(xg_ref.dtype)
    rf_ref[...] = _row_factor(_lane_group_sum(x * x), x.shape[1])


def _norm_prep(x, g, tm=512):
    m, d = x.shape
    return pl.pallas_call(
        _norm_prep_body,
        grid=(m // tm,),
        in_specs=[pl.BlockSpec((tm, d), lambda i: (i, 0)),
                  pl.BlockSpec((1, d), lambda i: (0, 0))],
        out_specs=[pl.BlockSpec((tm, d), lambda i: (i, 0)),
                   pl.BlockSpec((tm, LANES), lambda i: (i, 0))],
        out_shape=[jax.ShapeDtypeStruct((m, d), BF16), jax.ShapeDtypeStruct((m, LANES), F32)],
        compiler_params=_params("parallel"),
        name="norm_prep",
    )(x, g.reshape(1, d))


def _swiglu_cast_body(a_ref, wg_ref, wu_ref, rf_ref, wd_ref, o_ref, wdb_ref):
    a = a_ref[...]
    r = _row_scale(rf_ref, o_ref.shape[1])
    g = r * _dot(a, wg_ref[...].astype(BF16))
    u = r * _dot(a, wu_ref[...].astype(BF16))
    o_ref[...] = (g * jax.nn.sigmoid(g) * u).astype(o_ref.dtype)

    @pl.when(pl.program_id(0) == 0)
    def _():
        wdb_ref[...] = wd_ref[...].astype(wdb_ref.dtype)


def _mm_body(a_ref, w_ref, rf_ref, o_ref):
    r = _row_scale(rf_ref, o_ref.shape[1])
    o_ref[...] = (r * _dot(a_ref[...], w_ref[...].astype(BF16))).astype(o_ref.dtype)


def _mm_res_body(*refs, na, scale, norm_width):
    a_refs, (w_ref, r_ref), refs = refs[:na], refs[na:na + 2], refs[na + 2:]
    acc, k0 = None, 0
    for a_ref in a_refs:
        k1 = k0 + a_ref.shape[1]
        part = _dot(a_ref[...], w_ref[k0:k1, :].astype(BF16))
        acc, k0 = part if acc is None else acc + part, k1
    x = r_ref[...] + scale * acc
    if not norm_width:
        o_ref, = refs
        o_ref[...] = x
        return
    g_ref, o_ref, xg_ref, rf_ref = refs
    o_ref[...] = x
    xg_ref[...] = (x * g_ref[...]).astype(xg_ref.dtype)
    part = _lane_group_sum(x * x)
    j = pl.program_id(1)

    @pl.when(j == 0)
    def _():
        rf_ref[...] = part

    @pl.when(j > 0)
    def _():
        rf_ref[...] += part

    @pl.when(j == pl.num_programs(1) - 1)
    def _():
        rf_ref[...] = _row_factor(rf_ref[...], norm_width)


def _matmul(body, a_list, weights, out_dtype, tm, tn, name, *, row_factor=None, res=None,
            norm_next=None, cast_rows=None):
    m = a_list[0].shape[0]
    n = weights[0][0].shape[-1]
    k = sum(a.shape[1] for a in a_list)
    nj = n // tn
    tile = pl.BlockSpec((tm, tn), lambda i, j: (i, j))
    rows = pl.BlockSpec((tm, LANES), lambda i, j: (i, 0))
    in_specs = [pl.BlockSpec((tm, a.shape[1]), lambda i, j: (i, 0), pipeline_mode=pl.Buffered(1))
                for a in a_list]
    for w, prefix in weights:
        in_specs.append(pl.BlockSpec((None,) * len(prefix) + (k, tn),
                                     lambda i, j, prefix=prefix: prefix + (0, j)))
    operands = list(a_list) + [w for w, _ in weights]
    out_specs = [tile]
    out_shape = [jax.ShapeDtypeStruct((m, n), out_dtype)]
    if row_factor is not None:
        in_specs.append(rows)
        operands.append(row_factor)
    if res is not None:
        in_specs.append(tile)
        operands.append(res)
    if norm_next is not None:
        in_specs.append(pl.BlockSpec((1, tn), lambda i, j: (0, j)))
        operands.append(norm_next.reshape(1, n))
        out_specs += [tile, rows]
        out_shape += [jax.ShapeDtypeStruct((m, n), BF16), jax.ShapeDtypeStruct((m, LANES), F32)]
    if cast_rows is not None:
        w, prefix = cast_rows
        cols = w.shape[-1]
        row_block = lambda i, j: jnp.where(i == 0, j, nj - 1)
        in_specs.append(pl.BlockSpec((None,) * len(prefix) + (tn, cols),
                                     lambda i, j: prefix + (row_block(i, j), 0)))
        operands.append(w)
        out_specs.append(pl.BlockSpec((tn, cols), lambda i, j: (row_block(i, j), 0)))
        out_shape.append(jax.ShapeDtypeStruct((n, cols), BF16))
    return pl.pallas_call(
        body,
        grid=(m // tm, nj),
        in_specs=in_specs,
        out_specs=out_specs,
        out_shape=out_shape,
        compiler_params=_params("arbitrary", "arbitrary"),
        name=name,
    )(*operands)


def _attn_bias_table():
    il = np.arange(ATTN_TQ)[:, None]
    c = np.arange(ATTN_TQ + 2 * ATTN_REACH)[None, :]
    d = il + ATTN_REACH - c
    mult = np.zeros(d.shape, np.int64)
    for window, dil in DILATED_CONFIGS:
        half = window // (2 * dil)
        mult += ((d % dil) == 0) & (np.abs(d) <= half * dil)
    return np.where(mult > 0, np.log2(np.maximum(mult, 1)), NEG_INF).astype(np.float32)


def _attn_body(q_ref, k_ref, v_ref, gq_ref, gk_ref, cos_ref, sin_ref, bias_ref, go_ref,
               o_ref, qs_ref, ks_ref, vs_ref):
    seq = q_ref.shape[1]
    cos = cos_ref[...]
    sin = sin_ref[...]

    def norm_rot(x, g):
        y = x * lax.rsqrt(jnp.mean(x * x, axis=-1, keepdims=True) + RMS_EPS) * g
        return y * cos + pltpu.roll(y, HEAD_DIM // 2, axis=1) * sin

    q_scale = math.log2(math.e) / math.sqrt(HEAD_DIM)
    qs_ref[...] = (norm_rot(q_ref[0], gq_ref[...]) * q_scale).astype(BF16)
    ks_ref[...] = norm_rot(k_ref[0], gk_ref[...]).astype(BF16)
    vs_ref[:, 0:HEAD_DIM] = v_ref[0].astype(BF16)
    vs_ref[:, HEAD_DIM:] = jnp.ones((seq, LANES), BF16)

    for t in range(seq // ATTN_TQ):
        q0 = t * ATTN_TQ
        lo = max(0, q0 - ATTN_REACH)
        hi = min(seq, q0 + ATTN_TQ + ATTN_REACH)
        c0 = lo - (q0 - ATTN_REACH)
        s = lax.dot_general(qs_ref[q0:q0 + ATTN_TQ, :], ks_ref[lo:hi, :],
                            (((1,), (1,)), ((), ())), preferred_element_type=F32)
        s = s + bias_ref[:, c0:c0 + (hi - lo)]
        m = jnp.max(s, axis=-1, keepdims=True)
        p = jnp.exp2(s - m)
        acc = _dot(p.astype(BF16), vs_ref[lo:hi, :])
        o = acc[:, 0:HEAD_DIM] / acc[:, HEAD_DIM:]
        o = o * lax.rsqrt(jnp.mean(o * o, axis=-1, keepdims=True) + RMS_EPS) * go_ref[...]
        o_ref[0, q0:q0 + ATTN_TQ, :] = o.astype(o_ref.dtype)


def _attention(proj, gq, gk, cos, sin_signed, g_out):
    b, s, _ = proj.shape
    hd = HEAD_DIM
    bias = jnp.asarray(_attn_bias_table())
    head = lambda off: pl.BlockSpec((1, s, hd), lambda i, h: (i, 0, off + h))
    const2 = lambda shape: pl.BlockSpec(shape, lambda i, h: (0, 0))
    return pl.pallas_call(
        _attn_body,
        grid=(b, N_HEADS),
        in_specs=[head(0), head(N_HEADS), head(2 * N_HEADS),
                  const2((1, hd)), const2((1, hd)),
                  const2((s, hd)), const2((s, hd)),
                  const2(bias.shape),
                  pl.BlockSpec((1, hd), lambda i, h: (0, h))],
        out_specs=pl.BlockSpec((1, s, hd), lambda i, h: (i, 0, h)),
        out_shape=jax.ShapeDtypeStruct((b, s, ATTN_WIDTH), BF16),
        scratch_shapes=[pltpu.VMEM((s, hd), BF16)] * 2 + [pltpu.VMEM((s, hd + LANES), BF16)],
        compiler_params=_params("parallel", "arbitrary"),
        name="attention",
    )(proj, proj, proj, gq.reshape(1, hd), gk.reshape(1, hd), cos, sin_signed, bias,
      g_out.reshape(1, ATTN_WIDTH))


def _shortconv_eo(ue, uo, w_ref, b_ref):
    half, edge = ue.shape[0], 8
    w0, w1, w2, b = w_ref[0:1, :], w_ref[1:2, :], w_ref[2:3, :], b_ref[...]
    taps = lambda prev, cur, nxt: w0 * prev + w1 * cur + w2 * nxt + b
    uo_prev = pltpu.roll(uo, 1, axis=0)
    ue_next = pltpu.roll(ue, half - 1, axis=0)
    row = lax.broadcasted_iota(jnp.int32, (edge, ue.shape[1]), 0)
    lo, hi = slice(0, edge), slice(half - edge, half)
    even_top = taps(jnp.where(row == 0, 0.0, uo_prev[lo]), ue[lo], uo[lo])
    odd_bot = taps(ue[hi], uo[hi], jnp.where(row == edge - 1, 0.0, ue_next[hi]))
    even = jnp.concatenate([even_top, taps(uo_prev, ue, uo)[edge:]], axis=0)
    odd = jnp.concatenate([taps(ue, uo, ue_next)[:half - edge], odd_bot], axis=0)
    return even, odd


def _filt_hidden_body(emb_ref, w1_ref, b1_ref, w2_ref, b2_ref, fr_ref, o_ref):
    hp = lax.Precision.HIGHEST
    h = jnp.dot(emb_ref[...], w1_ref[...], precision=hp, preferred_element_type=F32)
    h = jnp.sin(fr_ref[0:1, :] * (h + b1_ref[...]))
    h = jnp.dot(h, w2_ref[...], precision=hp, preferred_element_type=F32)
    o_ref[...] = jnp.sin(fr_ref[1:2, :] * (h + b2_ref[...]))


def _filt_out_body(h_ref, w3_ref, t_ref, delta_ref, o_ref):
    h = jnp.dot(h_ref[...], w3_ref[...], precision=lax.Precision.HIGHEST,
                preferred_element_type=F32)
    o_ref[...] = h * jnp.exp(-t_ref[...] * delta_ref[...])


def _pad_to(a, shape):
    return jnp.pad(a, [(0, t - s) for s, t in zip(a.shape, shape)])


def _hyena_filters(w1, b1, w2, b2, freq, w3, tn=512):
    seq = SEQ
    t = jnp.linspace(0.0, 1.0, seq, dtype=F32)[:, None]
    bands = (FILTER_EMB_DIM - 1) // 2
    f = jnp.linspace(1e-4, bands - 1, bands, dtype=F32)[None, :]
    wpos = (2.0 * math.pi) * jnp.arange(seq, dtype=F32)[:, None] / seq
    emb = jnp.concatenate([t, jnp.cos(f * wpos), -jnp.sin(f * wpos)], axis=-1)
    deltas = jnp.abs(jnp.linspace(math.log(DECAY_FAST) / DECAY_TARGET,
                                  math.log(DECAY_SLOW) / DECAY_TARGET, HYENA_WIDTH, dtype=F32))
    ncol = w3.shape[1]
    deltas = jnp.tile(deltas, ncol // HYENA_WIDTH).reshape(1, ncol)
    hid = LANES
    hidden = pl.pallas_call(
        _filt_hidden_body,
        out_shape=jax.ShapeDtypeStruct((seq, hid), F32),
        name="filt_hidden",
    )(_pad_to(emb, (seq, hid)), _pad_to(w1, (hid, hid)), _pad_to(b1[None], (1, hid)),
      _pad_to(w2, (hid, hid)), _pad_to(b2[None], (1, hid)), _pad_to(freq, (2, hid)))
    return pl.pallas_call(
        _filt_out_body,
        grid=(ncol // tn,),
        in_specs=[pl.BlockSpec((seq, hid), lambda j: (0, 0)),
                  pl.BlockSpec((hid, tn), lambda j: (0, j)),
                  pl.BlockSpec((seq, 1), lambda j: (0, 0)),
                  pl.BlockSpec((1, tn), lambda j: (0, j))],
        out_specs=pl.BlockSpec((seq, tn), lambda j: (0, j)),
        out_shape=jax.ShapeDtypeStruct((seq, ncol), F32),
        compiler_params=_params("parallel"),
        name="filt_out",
    )(hidden, _pad_to(w3, (hid, ncol)), t, deltas)


def _dft_matrices():
    h = DFT_HALF
    def table(t, n):
        ang = ((t[:, None] * n[None, :]) & (DFT_N - 1)).astype(F32) * (2.0 * math.pi / DFT_N)
        return jnp.cos(ang), jnp.sin(ang)
    def cos_sin(n):
        c_hi, s_hi = (x[:, None, :] for x in table(jnp.arange(0, h, DFT_SPLIT, dtype=jnp.int32), n))
        c_lo, s_lo = (x[None, :, :] for x in table(jnp.arange(DFT_SPLIT, dtype=jnp.int32), n))
        return ((c_hi * c_lo - s_hi * s_lo).reshape(h, -1),
                (s_hi * c_lo + c_hi * s_lo).reshape(h, -1))
    even = 2 * jnp.arange(h, dtype=jnp.int32)
    (ce, se), (co, so) = cos_sin(even), cos_sin(even + 1)
    se, so = -se, -so
    alt = jnp.where(jnp.arange(h) % 2 == 0, 1.0, -1.0).astype(F32)
    first = jnp.arange(h) == 0
    fwd = [ce, co, jnp.where(first[:, None], alt[None, :], se),
           jnp.where(first[:, None], -alt[None, :], so)]
    wgt = jnp.where(first, 1.0 / DFT_N, 2.0 / DFT_N)[None, :]
    inv = [ce.T * wgt, jnp.where(first[None, :], alt[:, None] / DFT_N, se.T * wgt),
           co.T * wgt, jnp.where(first[None, :], -alt[:, None] / DFT_N, so.T * wgt)]
    return jnp.stack(fwd).astype(BF16), jnp.stack(inv).astype(BF16)


def _rdft(mats_ref, ze, zo):
    ze = ze.astype(BF16)
    zo = zo.astype(BF16)
    return (_dot(mats_ref[0], ze), _dot(mats_ref[1], zo), _dot(mats_ref[2], ze),
            _dot(mats_ref[3], zo))


def _first_row(shape):
    return lax.broadcasted_iota(jnp.int32, shape, 0) == 0


def _seq_specs(operand, lead, tn, index):
    def spec(rows, width, offset):
        def index_map(*g):
            pre, j = index(*g)
            return pre + (0, offset + j * (tn // width))
        return pl.BlockSpec((None,) * lead + (rows, width), index_map)
    if not _is_natural(operand):
        return [spec(DFT_HALF, tn, 0)] * 2, list(operand)
    array, col = operand
    groups = tn // LANES
    return [spec(SEQ, LANES, col // LANES + g) for g in range(groups)], [array] * groups


def _load_seq(refs, natural):
    if not natural:
        return refs[0][...], refs[1][...]
    parity = lambda p: jnp.concatenate([r[pl.ds(p, DFT_HALF, stride=2), :] for r in refs], axis=1)
    return parity(0), parity(1)


def _kspec_body(mats_ref, *refs):
    h = DFT_HALF
    *h_refs, hb0_ref, kre_ref, kim_ref = refs
    n = len(h_refs) // 2
    fer, for_, fei, foi = _rdft(mats_ref, *_load_seq(h_refs[:n], True))
    ber, bor, bei, boi = _rdft(mats_ref, *_load_seq(h_refs[n:], True))
    hb0 = hb0_ref[0:1, :]
    kre_ref[0:h, :] = (fer + for_) + (ber + bor) - hb0
    kre_ref[h:, :] = (fer - for_) + (ber - bor) - hb0
    kim_ref[0:h, :] = (fei + foi) - (bei + boi)
    kim_ref[h:, :] = (foi - fei) - (boi - bei)
    r = 8
    first = _first_row((r, fer.shape[1]))
    kim_ref[0:r, :] = jnp.where(first, fei[:r] + bei[:r] - hb0, (fei + foi - bei - boi)[:r])
    kim_ref[h:h + r, :] = jnp.where(first, foi[:r] - boi[:r], (foi - fei - boi + bei)[:r])


def _kspec(fwd, filt, tn=256):
    c = HYENA_WIDTH
    nc = c // tn
    out = jax.ShapeDtypeStruct((SEQ, HYENA_ORDER * c), F32)
    index = lambda o, j: ((), o * 2 * nc + j)
    hf_specs, hf_ops = _seq_specs((filt, 0), 0, tn, index)
    hb_specs, hb_ops = _seq_specs((filt, c), 0, tn, index)
    return pl.pallas_call(
        _kspec_body,
        grid=(HYENA_ORDER, nc),
        in_specs=[pl.BlockSpec(fwd.shape, lambda o, j: (0, 0, 0), pipeline_mode=pl.Buffered(1))]
        + hf_specs + hb_specs + [pl.BlockSpec((8, tn), lambda o, j: (0, o * 2 * nc + nc + j))],
        out_specs=[pl.BlockSpec((SEQ, tn), lambda o, j: (0, o * nc + j))] * 2,
        out_shape=[out, out],
        compiler_params=_params("parallel", "arbitrary"),
        name="kspec",
    )(fwd, *hf_ops, *hb_ops, filt)


def _dft_fwd_body(mats_ref, *refs, nz, natural, conv):
    ze, zo = _load_seq(refs[:nz], natural)
    if conv:
        ze, zo = _shortconv_eo(ze, zo, *refs[nz:nz + 2])
    kre_ref, kim_ref, yre_ref, yim_ref = refs[-4:]
    h = DFT_HALF
    er, or_, ei, oi = _rdft(mats_ref, ze, zo)
    out = yre_ref.dtype
    for rows, zr, zi in ((slice(0, h), er + or_, ei + oi), (slice(h, 2 * h), er - or_, oi - ei)):
        kr, ki = kre_ref[rows, :], kim_ref[rows, :]
        yre_ref[rows, :] = (zr * kr - zi * ki).astype(out)
        yim_ref[rows, :] = (zr * ki + zi * kr).astype(out)
    r = 16
    first = _first_row((r, er.shape[1]))
    e, o, p, q = er[:r], or_[:r], ei[:r], oi[:r]
    krt, krb, kit, kib = kre_ref[0:r, :], kre_ref[h:h + r, :], kim_ref[0:r, :], kim_ref[h:h + r, :]
    re4 = p * kit - q * kib
    im4 = p * kib + q * kit
    yre_ref[0:r, :] = jnp.where(first, (e + o) * krt, (e + o) * krt - (p + q) * kit).astype(out)
    yim_ref[0:r, :] = jnp.where(first, re4 + im4, (e + o) * kit + (p + q) * krt).astype(out)
    yre_ref[h:h + r, :] = jnp.where(first, (e - o) * krb, (e - o) * krb - (q - p) * kib).astype(out)
    yim_ref[h:h + r, :] = jnp.where(first, im4 - re4, (e - o) * kib + (q - p) * krb).astype(out)


def _is_natural(operand):
    return isinstance(operand[1], int)


def _dft_fwd(fwd, z, conv, kre, kim, order, tn=512):
    b = z[0].shape[0]
    c = HYENA_WIDTH
    nc = c // tn
    out = jax.ShapeDtypeStruct((b, SEQ, c), BF16)
    z_specs, z_ops = _seq_specs(z, 1, tn, lambda j, i: ((i,), j))
    in_specs = [pl.BlockSpec(fwd.shape, lambda j, i: (0, 0, 0), pipeline_mode=pl.Buffered(1))]
    in_specs += z_specs
    operands = [fwd] + z_ops
    if conv is not None:
        cw, cb, col = conv
        in_specs += [pl.BlockSpec((cw.shape[0], tn), lambda j, i: (0, col // tn + j)),
                     pl.BlockSpec((1, tn), lambda j, i: (0, col // tn + j))]
        operands += [cw, cb]
    in_specs += [pl.BlockSpec((SEQ, tn), lambda j, i: (0, order * nc + j),
                              pipeline_mode=pl.Buffered(1))] * 2
    operands += [kre, kim]
    return pl.pallas_call(
        functools.partial(_dft_fwd_body, nz=len(z_ops), natural=_is_natural(z),
                          conv=conv is not None),
        grid=(nc, b),
        in_specs=in_specs,
        out_specs=[pl.BlockSpec((None, SEQ, tn), lambda j, i: (i, 0, j))] * 2,
        out_shape=[out, out],
        compiler_params=_params("parallel", "arbitrary"),
        name="dft_fwd",
    )(*operands)


def _dft_inv_body(mats_ref, yre_ref, yim_ref, *refs, nz, z_natural, conv_z, ng, final):
    ze, zo = _load_seq(refs[:nz], z_natural)
    refs = refs[nz:]
    if conv_z:
        ze, zo = _shortconv_eo(ze, zo, *refs[:2])
        refs = refs[2:]
    ge, go = _shortconv_eo(*_load_seq(refs[:ng], True), *refs[ng:ng + 2])
    b_ref, gn_ref, *outs = refs[ng + 2:]
    h = DFT_HALF
    yrt, yrb = yre_ref[0:h, :].astype(F32), yre_ref[h:, :].astype(F32)
    yit, yib = yim_ref[0:h, :].astype(F32), yim_ref[h:, :].astype(F32)
    ye = _dot(mats_ref[0], (yrt + yrb).astype(BF16)) + _dot(mats_ref[1], (yit - yib).astype(BF16))
    yo = _dot(mats_ref[2], (yrt - yrb).astype(BF16)) + _dot(mats_ref[3], (yit + yib).astype(BF16))
    bias = b_ref[...]
    halves = (ge * (ye + bias * ze), go * (yo + bias * zo))
    if not final:
        for o_ref, zn in zip(outs, halves):
            o_ref[...] = zn
        return
    o_ref, stage_ref = outs
    for gi in range(halves[0].shape[1] // HYENA_GROUP):
        sl = slice(gi * HYENA_GROUP, (gi + 1) * HYENA_GROUP)
        for parity, zn in enumerate(halves):
            blk = zn[:, sl]
            blk = blk * lax.rsqrt(jnp.mean(blk * blk, axis=-1, keepdims=True) + RMS_EPS)
            stage_ref[pl.ds(parity, h, stride=2), :] = blk * gn_ref[:, sl]
        o_ref[:, sl] = stage_ref[...].astype(o_ref.dtype)


def _dft_inv(inv, yre, yim, z, conv_z, gate, conv_g, bias, gnorm, final, tn=512):
    b = yre.shape[0]
    c = HYENA_WIDTH
    index = lambda j, i: ((i,), j)
    taps = lambda cw, col: [pl.BlockSpec((cw.shape[0], tn), lambda j, i: (0, col // tn + j)),
                            pl.BlockSpec((1, tn), lambda j, i: (0, col // tn + j))]
    z_specs, z_ops = _seq_specs(z, 1, tn, index)
    g_specs, g_ops = _seq_specs(gate, 1, tn, index)
    in_specs = [pl.BlockSpec(inv.shape, lambda j, i: (0, 0, 0), pipeline_mode=pl.Buffered(1)),
                pl.BlockSpec((None, SEQ, tn), lambda j, i: (i, 0, j)),
                pl.BlockSpec((None, SEQ, tn), lambda j, i: (i, 0, j))] + z_specs
    operands = [inv, yre, yim] + z_ops
    if conv_z is not None:
        in_specs += taps(conv_z[0], conv_z[2])
        operands += list(conv_z[:2])
    in_specs += g_specs + taps(conv_g[0], conv_g[2])
    in_specs += [pl.BlockSpec((1, tn), lambda j, i: (0, j))] * 2
    operands += g_ops + [conv_g[0], conv_g[1], bias.reshape(1, c), gnorm.reshape(1, c)]
    if final:
        out_specs = pl.BlockSpec((None, SEQ, tn), lambda j, i: (i, 0, j))
        out_shape = jax.ShapeDtypeStruct((b, SEQ, c), BF16)
        scratch = [pltpu.VMEM((SEQ, HYENA_GROUP), F32)]
    else:
        out_specs = [pl.BlockSpec((None, DFT_HALF, tn), lambda j, i: (i, 0, j))] * 2
        out_shape = [jax.ShapeDtypeStruct((b, DFT_HALF, c), F32)] * 2
        scratch = []
    return pl.pallas_call(
        functools.partial(_dft_inv_body, nz=len(z_ops), z_natural=_is_natural(z),
                          conv_z=conv_z is not None, ng=len(g_ops), final=final),
        grid=(c // tn, b),
        in_specs=in_specs,
        out_specs=out_specs,
        out_shape=out_shape,
        scratch_shapes=scratch,
        compiler_params=_params("parallel", "arbitrary"),
        name="dft_inv",
    )(*operands)


def _hyena(proj, dft, conv_w, conv_b, w1, b1, w2, b2, freq, w3, filt_bias, g_out):
    fwd, inv = dft
    c = HYENA_WIDTH
    raw = IN_COLS - (HYENA_ORDER + 1) * c
    conv_b = conv_b.reshape(1, -1)
    filt = _hyena_filters(w1, b1, w2, b2, freq, w3)
    kre, kim = _kspec(fwd, filt)
    z, conv_z = (proj, raw), (conv_w, conv_b, 0)
    for order in range(HYENA_ORDER):
        final = order == HYENA_ORDER - 1
        yre, yim = _dft_fwd(fwd, z, conv_z, kre, kim, order)
        z = _dft_inv(inv, yre, yim, z, conv_z, (proj, raw + (order + 1) * c),
                     (conv_w, conv_b, (order + 1) * c), filt_bias[order], g_out, final)
        conv_z = None
    return z


def _rotary_tables():
    half = HEAD_DIM // 2
    pos = jnp.arange(SEQ, dtype=F32)
    inv = ROPE_THETA ** (-jnp.arange(half, dtype=F32) / half)
    ang = pos[:, None] * inv[None, :]
    cos = jnp.cos(ang)
    sin = jnp.sin(ang)
    return jnp.concatenate([cos, cos], axis=-1), jnp.concatenate([-sin, sin], axis=-1)


def _mixer(proj, dft, rot, q_norm, k_norm, conv_w, conv_b, w1, b1, w2, b2, freq, w3,
           filt_bias, attn_out_norm, hyena_out_norm):
    attn = _attention(proj, q_norm, k_norm, rot[0], rot[1], attn_out_norm)
    hy = _hyena(proj, dft, conv_w, conv_b, w1, b1, w2, b2, freq, w3, filt_bias, hyena_out_norm)
    return attn, hy


def _residual_matmul(a_list, weight, x, scale, norm_next, tm, name):
    width = weight[0].shape[-1]
    body = functools.partial(_mm_res_body, na=len(a_list), scale=scale,
                             norm_width=0 if norm_next is None else width)
    x, *normed = _matmul(body, a_list, [weight], F32, tm, 256, name, res=x, norm_next=norm_next)
    return x, normed


def _ffn(x, normed, w_gate, w_up, w_down, idx, norm_next):
    xg, rf = normed
    a, wd = _matmul(_swiglu_cast_body, [xg], [(w_gate, idx), (w_up, idx)], BF16, 2048, 256,
                    "ffn_up", row_factor=rf, cast_rows=(w_down, idx))
    return _residual_matmul([a], (wd, ()), x, 0.5, norm_next, 1024, "ffn_down")


def kernel(x, ffn_norm, ffn_w_gate, ffn_w_up, ffn_w_down, mix_norm, w_in, q_norm, k_norm, conv_w, conv_b, filt_w1, filt_b1, filt_w2, filt_b2, filt_freq, filt_w3, filt_bias, attn_out_norm, hyena_out_norm, w_out):
    b, s, d = x.shape
    x = x.reshape(b * s, d)
    dft = _dft_matrices()
    rot = _rotary_tables()
    normed = _norm_prep(x, ffn_norm[0, 0])
    for l in range(DEPTH):
        x, normed = _ffn(x, normed, ffn_w_gate, ffn_w_up, ffn_w_down, (l, 0), mix_norm[l])
        proj, = _matmul(_mm_body, [normed[0]], [(w_in, (l,))], F32, 2048, 256, "in_proj",
                        row_factor=normed[1])
        attn, hy = _mixer(proj.reshape(b, s, IN_COLS), dft, rot, q_norm[l], k_norm[l], conv_w[l],
                          conv_b[l], filt_w1[l], filt_b1[l], filt_w2[l], filt_b2[l], filt_freq[l],
                          filt_w3[l], filt_bias[l], attn_out_norm[l], hyena_out_norm[l])
        x, normed = _residual_matmul([attn.reshape(b * s, -1), hy.reshape(b * s, -1)],
                                     (w_out, (l,)), x, 1.0, ffn_norm[l, 1], 2048, "out_proj")
        last = l == DEPTH - 1
        x, normed = _ffn(x, normed, ffn_w_gate, ffn_w_up, ffn_w_down, (l, 1),
                         None if last else ffn_norm[l + 1, 0])
    return x.reshape(b, s, d)
```

```python
import functools
import math

import numpy as np
import jax
import jax.numpy as jnp
from jax import lax
from jax.experimental import pallas as pl
from jax.experimental.pallas import tpu as pltpu

D_MODEL = 4096
BATCH = 4
SEQ = 2048
DEPTH = 2
ATTN_WIDTH = D_MODEL // 2
HYENA_WIDTH = D_MODEL - ATTN_WIDTH
HEAD_DIM = 128
N_HEADS = ATTN_WIDTH // HEAD_DIM
DILATED_CONFIGS = ((128, 1), (512, 4), (2048, 16))
ROPE_THETA = 10000.0
HYENA_ORDER = 2
HYENA_GROUP = 128
FILTER_EMB_DIM = 33
FILTER_HIDDEN = 64
DECAY_FAST = 0.3
DECAY_SLOW = 1.5
DECAY_TARGET = 1e-2
FFN_HIDDEN = 256 * (-(-(8 * D_MODEL) // (3 * 256)))
IN_COLS = 3 * ATTN_WIDTH + (HYENA_ORDER + 1) * HYENA_WIDTH
RMS_EPS = 1e-6
NEG_INF = -1e30

F32 = jnp.float32
BF16 = jnp.bfloat16

V7X_VMEM_LIMIT_BYTES = 56 * 1024 * 1024
LANES = 128
ATTN_TQ = 256
ATTN_FAR_CONFIG = max(DILATED_CONFIGS, key=lambda wd: wd[1])
ATTN_NEAR_CONFIGS = tuple(c for c in DILATED_CONFIGS if c != ATTN_FAR_CONFIG)
ATTN_NEAR_REACH = max((w // (2 * d)) * d for w, d in ATTN_NEAR_CONFIGS)
DFT_N = 2 * SEQ
DFT_HALF = SEQ // 2
DFT_SPLIT = 32


def _params(*sem):
    return pltpu.CompilerParams(dimension_semantics=sem, vmem_limit_bytes=V7X_VMEM_LIMIT_BYTES)


def _dot(a, b):
    return jnp.dot(a, b, preferred_element_type=F32)


def _lane_group_sum(v):
    out = v[:, 0:LANES]
    for k in range(1, v.shape[1] // LANES):
        out = out + v[:, k * LANES:(k + 1) * LANES]
    return out


def _row_factor(lane_sums, width):
    ssq = jnp.sum(lane_sums, axis=-1, keepdims=True)
    return jnp.broadcast_to(lax.rsqrt(ssq / width + RMS_EPS), lane_sums.shape)


def _row_scale(rf_ref, width):
    return jnp.concatenate([rf_ref[...]] * (width // LANES), axis=1)


def _norm_prep_body(x_ref, g_ref, xg_ref, rf_ref):
    x = x_ref[...]
    xg_ref[...] = (x * g_ref[...]).astype(xg_ref.dtype)
    rf_ref[...] = _row_factor(_lane_group_sum(x * x), x.shape[1])


def _norm_prep(x, g, tm=512):
    m, d = x.shape
    return pl.pallas_call(
        _norm_prep_body,
        grid=(m // tm,),
        in_specs=[pl.BlockSpec((tm, d), lambda i: (i, 0)),
                  pl.BlockSpec((1, d), lambda i: (0, 0))],
        out_specs=[pl.BlockSpec((tm, d), lambda i: (i, 0)),
                   pl.BlockSpec((tm, LANES), lambda i: (i, 0))],
        out_shape=[jax.ShapeDtypeStruct((m, d), BF16), jax.ShapeDtypeStruct((m, LANES), F32)],
        compiler_params=_params("parallel"),
        name="norm_prep",
    )(x, g.reshape(1, d))


def _swiglu_cast_body(a_ref, wg_ref, wu_ref, rf_ref, wd_ref, o_ref, wdb_ref):
    a = a_ref[...]
    r = _row_scale(rf_ref, o_ref.shape[1])
    g = r * _dot(a, wg_ref[...].astype(BF16))
    u = r * _dot(a, wu_ref[...].astype(BF16))
    o_ref[...] = (g * jax.nn.sigmoid(g) * u).astype(o_ref.dtype)

    @pl.when(pl.program_id(0) == 0)
    def _():
        wdb_ref[...] = wd_ref[...].astype(wdb_ref.dtype)


def _mm_body(a_ref, w_ref, rf_ref, o_ref):
    r = _row_scale(rf_ref, o_ref.shape[1])
    o_ref[...] = (r * _dot(a_ref[...], w_ref[...].astype(BF16))).astype(o_ref.dtype)


def _mm_res_body(*refs, na, scale, norm_width):
    a_refs, (w_ref, r_ref), refs = refs[:na], refs[na:na + 2], refs[na + 2:]
    acc, k0 = None, 0
    for a_ref in a_refs:
        k1 = k0 + a_ref.shape[1]
        part = _dot(a_ref[...], w_ref[k0:k1, :].astype(BF16))
        acc, k0 = part if acc is None else acc + part, k1
    x = r_ref[...] + scale * acc
    if not norm_width:
        o_ref, = refs
        o_ref[...] = x
        return
    g_ref, o_ref, xg_ref, rf_ref = refs
    o_ref[...] = x
    xg_ref[...] = (x * g_ref[...]).astype(xg_ref.dtype)
    part = _lane_group_sum(x * x)
    j = pl.program_id(1)

    @pl.when(j == 0)
    def _():
        rf_ref[...] = part

    @pl.when(j > 0)
    def _():
        rf_ref[...] += part

    @pl.when(j == pl.num_programs(1) - 1)
    def _():
        rf_ref[...] = _row_factor(rf_ref[...], norm_width)


def _matmul(body, a_list, weights, out_dtype, tm, tn, name, *, row_factor=None, res=None,
            norm_next=None, cast_rows=None):
    m = a_list[0].shape[0]
    n = weights[0][0].shape[-1]
    k = sum(a.shape[1] for a in a_list)
    nj = n // tn
    tile = pl.BlockSpec((tm, tn), lambda i, j: (i, j))
    rows = pl.BlockSpec((tm, LANES), lambda i, j: (i, 0))
    in_specs = [pl.BlockSpec((tm, a.shape[1]), lambda i, j: (i, 0), pipeline_mode=pl.Buffered(1))
                for a in a_list]
    for w, prefix in weights:
        in_specs.append(pl.BlockSpec((None,) * len(prefix) + (k, tn),
                                     lambda i, j, prefix=prefix: prefix + (0, j)))
    operands = list(a_list) + [w for w, _ in weights]
    out_specs = [tile]
    out_shape = [jax.ShapeDtypeStruct((m, n), out_dtype)]
    if row_factor is not None:
        in_specs.append(rows)
        operands.append(row_factor)
    if res is not None:
        in_specs.append(tile)
        operands.append(res)
    if norm_next is not None:
        in_specs.append(pl.BlockSpec((1, tn), lambda i, j: (0, j)))
        operands.append(norm_next.reshape(1, n))
        out_specs += [tile, rows]
        out_shape += [jax.ShapeDtypeStruct((m, n), BF16), jax.ShapeDtypeStruct((m, LANES), F32)]
    if cast_rows is not None:
        w, prefix = cast_rows
        cols = w.shape[-1]
        row_block = lambda i, j: jnp.where(i == 0, j, nj - 1)
        in_specs.append(pl.BlockSpec((None,) * len(prefix) + (tn, cols),
                                     lambda i, j: prefix + (row_block(i, j), 0)))
        operands.append(w)
        out_specs.append(pl.BlockSpec((tn, cols), lambda i, j: (row_block(i, j), 0)))
        out_shape.append(jax.ShapeDtypeStruct((n, cols), BF16))
    return pl.pallas_call(
        body,
        grid=(m // tm, nj),
        in_specs=in_specs,
        out_specs=out_specs,
        out_shape=out_shape,
        compiler_params=_params("arbitrary", "arbitrary"),
        name=name,
    )(*operands)


def _attn_near_bias():
    il = np.arange(ATTN_TQ)[:, None]
    c = np.arange(ATTN_TQ + 2 * ATTN_NEAR_REACH)[None, :]
    d = il + ATTN_NEAR_REACH - c
    mult = np.zeros(d.shape, np.int64)
    for window, dil in ATTN_NEAR_CONFIGS:
        half = window // (2 * dil)
        mult += ((d % dil) == 0) & (np.abs(d) <= half * dil)
    return np.where(mult > 0, np.log2(np.maximum(mult, 1)), NEG_INF).astype(np.float32)


def _attn_far_bias():
    window, dil = ATTN_FAR_CONFIG
    t = np.arange(SEQ // dil)
    band = np.abs(t[:, None] - t[None, :]) <= window // (2 * dil)
    return np.where(band, 0.0, NEG_INF).astype(np.float32)


def _attn_body(q_ref, k_ref, v_ref, gq_ref, gk_ref, cos_ref, sin_ref, near_ref, far_ref, go_ref,
               o_ref, qs_ref, ks_ref, vs_ref, qf_ref, kf_ref, far_acc_ref, far_den_ref, far_max_ref):
    seq = q_ref.shape[1]
    hd = HEAD_DIM
    cos = cos_ref[...]
    sin = sin_ref[...]

    def norm_rot(x, g):
        y = x * lax.rsqrt(jnp.mean(x * x, axis=-1, keepdims=True) + RMS_EPS) * g
        return y * cos + pltpu.roll(y, hd // 2, axis=1) * sin

    q_scale = math.log2(math.e) / math.sqrt(hd)
    qf_ref[...] = norm_rot(q_ref[0], gq_ref[...]) * q_scale
    kf_ref[...] = norm_rot(k_ref[0], gk_ref[...])
    qs_ref[...] = qf_ref[...].astype(BF16)
    ks_ref[...] = kf_ref[...].astype(BF16)
    vs_ref[:, 0:hd] = v_ref[0].astype(BF16)
    vs_ref[:, hd:] = jnp.ones((seq, LANES), BF16)

    def scores(q, k, bias):
        return lax.dot_general(q, k, (((1,), (1,)), ((), ())), preferred_element_type=F32) + bias

    def probabilities(s):
        m = jnp.max(s, axis=-1, keepdims=True)
        return jnp.exp2(s - m).astype(BF16), m

    dil = ATTN_FAR_CONFIG[1]
    members = seq // dil
    classes = [pl.ds(r, members, stride=dil) for r in range(dil)]
    far_bias = far_ref[...]
    s = jnp.concatenate([scores(qf_ref[cls, :].astype(BF16), kf_ref[cls, :].astype(BF16), far_bias)
                         for cls in classes], axis=0)
    p, m = probabilities(s)
    ones = jnp.ones((members, LANES), BF16)
    for r, cls in enumerate(classes):
        block = slice(r * members, (r + 1) * members)
        v_ones = jnp.concatenate([v_ref[0, cls, :].astype(BF16), ones], axis=1)
        acc = _dot(p[block], v_ones)
        far_acc_ref[cls, :] = acc[:, 0:hd]
        far_den_ref[cls, :] = acc[:, hd:]
        far_max_ref[cls, :] = jnp.broadcast_to(m[block], (members, LANES))

    for t in range(seq // ATTN_TQ):
        q0 = t * ATTN_TQ
        rows = slice(q0, q0 + ATTN_TQ)
        lo = max(0, q0 - ATTN_NEAR_REACH)
        hi = min(seq, q0 + ATTN_TQ + ATTN_NEAR_REACH)
        c0 = lo - (q0 - ATTN_NEAR_REACH)
        p, m_near = probabilities(scores(qs_ref[rows, :], ks_ref[lo:hi, :],
                                         near_ref[:, c0:c0 + (hi - lo)]))
        acc = _dot(p, vs_ref[lo:hi, :])
        acc, den = acc[:, 0:hd], acc[:, hd:]
        m_far = far_max_ref[rows, :]
        m = jnp.maximum(m_near, m_far)
        w_near = jnp.exp2(m_near - m)
        w_far = jnp.exp2(m_far - m)
        o = ((acc * w_near + far_acc_ref[rows, :] * w_far)
             / (den * w_near + far_den_ref[rows, :] * w_far))
        o = o * lax.rsqrt(jnp.mean(o * o, axis=-1, keepdims=True) + RMS_EPS) * go_ref[...]
        o_ref[0, rows, :] = o.astype(o_ref.dtype)


def _attention(proj, gq, gk, cos, sin_signed, g_out):
    b, s, _ = proj.shape
    hd = HEAD_DIM
    near = jnp.asarray(_attn_near_bias())
    far = jnp.asarray(_attn_far_bias())
    head = lambda off: pl.BlockSpec((1, s, hd), lambda i, h: (i, 0, off + h))
    const2 = lambda shape: pl.BlockSpec(shape, lambda i, h: (0, 0))
    return pl.pallas_call(
        _attn_body,
        grid=(b, N_HEADS),
        in_specs=[head(0), head(N_HEADS), head(2 * N_HEADS),
                  const2((1, hd)), const2((1, hd)),
                  const2((s, hd)), const2((s, hd)),
                  const2(near.shape), const2(far.shape),
                  pl.BlockSpec((1, hd), lambda i, h: (0, h))],
        out_specs=pl.BlockSpec((1, s, hd), lambda i, h: (i, 0, h)),
        out_shape=jax.ShapeDtypeStruct((b, s, ATTN_WIDTH), BF16),
        scratch_shapes=[pltpu.VMEM((s, hd), BF16)] * 2 + [pltpu.VMEM((s, hd + LANES), BF16)]
        + [pltpu.VMEM((s, hd), F32)] * 5,
        compiler_params=_params("parallel", "arbitrary"),
        name="attention",
    )(proj, proj, proj, gq.reshape(1, hd), gk.reshape(1, hd), cos, sin_signed, near, far,
      g_out.reshape(1, ATTN_WIDTH))


def _shortconv_eo(ue, uo, w_ref, b_ref):
    half, edge = ue.shape[0], 8
    w0, w1, w2, b = w_ref[0:1, :], w_ref[1:2, :], w_ref[2:3, :], b_ref[...]
    taps = lambda prev, cur, nxt: w0 * prev + w1 * cur + w2 * nxt + b
    uo_prev = pltpu.roll(uo, 1, axis=0)
    ue_next = pltpu.roll(ue, half - 1, axis=0)
    row = lax.broadcasted_iota(jnp.int32, (edge, ue.shape[1]), 0)
    lo, hi = slice(0, edge), slice(half - edge, half)
    even_top = taps(jnp.where(row == 0, 0.0, uo_prev[lo]), ue[lo], uo[lo])
    odd_bot = taps(ue[hi], uo[hi], jnp.where(row == edge - 1, 0.0, ue_next[hi]))
    even = jnp.concatenate([even_top, taps(uo_prev, ue, uo)[edge:]], axis=0)
    odd = jnp.concatenate([taps(ue, uo, ue_next)[:half - edge], odd_bot], axis=0)
    return even, odd


def _filt_hidden_body(emb_ref, w1_ref, b1_ref, w2_ref, b2_ref, fr_ref, o_ref):
    hp = lax.Precision.HIGHEST
    h = jnp.dot(emb_ref[...], w1_ref[...], precision=hp, preferred_element_type=F32)
    h = jnp.sin(fr_ref[0:1, :] * (h + b1_ref[...]))
    h = jnp.dot(h, w2_ref[...], precision=hp, preferred_element_type=F32)
    o_ref[...] = jnp.sin(fr_ref[1:2, :] * (h + b2_ref[...]))


def _filt_out_body(h_ref, w3_ref, t_ref, delta_ref, o_ref):
    h = jnp.dot(h_ref[...], w3_ref[...], precision=lax.Precision.HIGHEST,
                preferred_element_type=F32)
    o_ref[...] = h * jnp.exp(-t_ref[...] * delta_ref[...])


def _pad_to(a, shape):
    return jnp.pad(a, [(0, t - s) for s, t in zip(a.shape, shape)])


def _hyena_filters(w1, b1, w2, b2, freq, w3, tn=512):
    seq = SEQ
    t = jnp.linspace(0.0, 1.0, seq, dtype=F32)[:, None]
    bands = (FILTER_EMB_DIM - 1) // 2
    f = jnp.linspace(1e-4, bands - 1, bands, dtype=F32)[None, :]
    wpos = (2.0 * math.pi) * jnp.arange(seq, dtype=F32)[:, None] / seq
    emb = jnp.concatenate([t, jnp.cos(f * wpos), -jnp.sin(f * wpos)], axis=-1)
    deltas = jnp.abs(jnp.linspace(math.log(DECAY_FAST) / DECAY_TARGET,
                                  math.log(DECAY_SLOW) / DECAY_TARGET, HYENA_WIDTH, dtype=F32))
    ncol = w3.shape[1]
    deltas = jnp.tile(deltas, ncol // HYENA_WIDTH).reshape(1, ncol)
    hid = LANES
    hidden = pl.pallas_call(
        _filt_hidden_body,
        out_shape=jax.ShapeDtypeStruct((seq, hid), F32),
        name="filt_hidden",
    )(_pad_to(emb, (seq, hid)), _pad_to(w1, (hid, hid)), _pad_to(b1[None], (1, hid)),
      _pad_to(w2, (hid, hid)), _pad_to(b2[None], (1, hid)), _pad_to(freq, (2, hid)))
    return pl.pallas_call(
        _filt_out_body,
        grid=(ncol // tn,),
        in_specs=[pl.BlockSpec((seq, hid), lambda j: (0, 0)),
                  pl.BlockSpec((hid, tn), lambda j: (0, j)),
                  pl.BlockSpec((seq, 1), lambda j: (0, 0)),
                  pl.BlockSpec((1, tn), lambda j: (0, j))],
        out_specs=pl.BlockSpec((seq, tn), lambda j: (0, j)),
        out_shape=jax.ShapeDtypeStruct((seq, ncol), F32),
        compiler_params=_params("parallel"),
        name="filt_out",
    )(hidden, _pad_to(w3, (hid, ncol)), t, deltas)


def _dft_matrices():
    h = DFT_HALF
    def table(t, n):
        ang = ((t[:, None] * n[None, :]) & (DFT_N - 1)).astype(F32) * (2.0 * math.pi / DFT_N)
        return jnp.cos(ang), jnp.sin(ang)
    def cos_sin(n):
        c_hi, s_hi = (x[:, None, :] for x in table(jnp.arange(0, h, DFT_SPLIT, dtype=jnp.int32), n))
        c_lo, s_lo = (x[None, :, :] for x in table(jnp.arange(DFT_SPLIT, dtype=jnp.int32), n))
        return ((c_hi * c_lo - s_hi * s_lo).reshape(h, -1),
                (s_hi * c_lo + c_hi * s_lo).reshape(h, -1))
    even = 2 * jnp.arange(h, dtype=jnp.int32)
    (ce, se), (co, so) = cos_sin(even), cos_sin(even + 1)
    se, so = -se, -so
    alt = jnp.where(jnp.arange(h) % 2 == 0, 1.0, -1.0).astype(F32)
    first = jnp.arange(h) == 0
    fwd = [ce, co, jnp.where(first[:, None], alt[None, :], se),
           jnp.where(first[:, None], -alt[None, :], so)]
    wgt = jnp.where(first, 1.0 / DFT_N, 2.0 / DFT_N)[None, :]
    inv = [ce.T * wgt, jnp.where(first[None, :], alt[:, None] / DFT_N, se.T * wgt),
           co.T * wgt, jnp.where(first[None, :], -alt[:, None] / DFT_N, so.T * wgt)]
    return jnp.stack(fwd).astype(BF16), jnp.stack(inv).astype(BF16)


def _rdft(mats_ref, ze, zo):
    ze = ze.astype(BF16)
    zo = zo.astype(BF16)
    return (_dot(mats_ref[0], ze), _dot(mats_ref[1], zo), _dot(mats_ref[2], ze),
            _dot(mats_ref[3], zo))


def _first_row(shape):
    return lax.broadcasted_iota(jnp.int32, shape, 0) == 0


def _seq_specs(operand, lead, tn, index):
    def spec(rows, width, offset):
        def index_map(*g):
            pre, j = index(*g)
            return pre + (0, offset + j * (tn // width))
        return pl.BlockSpec((None,) * lead + (rows, width), index_map)
    if not _is_natural(operand):
        return [spec(DFT_HALF, tn, 0)] * 2, list(operand)
    array, col = operand
    groups = tn // LANES
    return [spec(SEQ, LANES, col // LANES + g) for g in range(groups)], [array] * groups


def _load_seq(refs, natural):
    if not natural:
        return refs[0][...], refs[1][...]
    parity = lambda p: jnp.concatenate([r[pl.ds(p, DFT_HALF, stride=2), :] for r in refs], axis=1)
    return parity(0), parity(1)


def _kspec_body(mats_ref, *refs):
    h = DFT_HALF
    *h_refs, hb0_ref, kre_ref, kim_ref = refs
    n = len(h_refs) // 2
    fer, for_, fei, foi = _rdft(mats_ref, *_load_seq(h_refs[:n], True))
    ber, bor, bei, boi = _rdft(mats_ref, *_load_seq(h_refs[n:], True))
    hb0 = hb0_ref[0:1, :]
    kre_ref[0:h, :] = (fer + for_) + (ber + bor) - hb0
    kre_ref[h:, :] = (fer - for_) + (ber - bor) - hb0
    kim_ref[0:h, :] = (fei + foi) - (bei + boi)
    kim_ref[h:, :] = (foi - fei) - (boi - bei)
    r = 8
    first = _first_row((r, fer.shape[1]))
    kim_ref[0:r, :] = jnp.where(first, fei[:r] + bei[:r] - hb0, (fei + foi - bei - boi)[:r])
    kim_ref[h:h + r, :] = jnp.where(first, foi[:r] - boi[:r], (foi - fei - boi + bei)[:r])


def _kspec(fwd, filt, tn=256):
    c = HYENA_WIDTH
    nc = c // tn
    out = jax.ShapeDtypeStruct((SEQ, HYENA_ORDER * c), F32)
    index = lambda o, j: ((), o * 2 * nc + j)
    hf_specs, hf_ops = _seq_specs((filt, 0), 0, tn, index)
    hb_specs, hb_ops = _seq_specs((filt, c), 0, tn, index)
    return pl.pallas_call(
        _kspec_body,
        grid=(HYENA_ORDER, nc),
        in_specs=[pl.BlockSpec(fwd.shape, lambda o, j: (0, 0, 0), pipeline_mode=pl.Buffered(1))]
        + hf_specs + hb_specs + [pl.BlockSpec((8, tn), lambda o, j: (0, o * 2 * nc + nc + j))],
        out_specs=[pl.BlockSpec((SEQ, tn), lambda o, j: (0, o * nc + j))] * 2,
        out_shape=[out, out],
        compiler_params=_params("parallel", "arbitrary"),
        name="kspec",
    )(fwd, *hf_ops, *hb_ops, filt)


def _dft_fwd_body(mats_ref, *refs, nz, natural, conv):
    ze, zo = _load_seq(refs[:nz], natural)
    if conv:
        ze, zo = _shortconv_eo(ze, zo, *refs[nz:nz + 2])
    kre_ref, kim_ref, yre_ref, yim_ref = refs[-4:]
    h = DFT_HALF
    er, or_, ei, oi = _rdft(mats_ref, ze, zo)
    out = yre_ref.dtype
    for rows, zr, zi in ((slice(0, h), er + or_, ei + oi), (slice(h, 2 * h), er - or_, oi - ei)):
        kr, ki = kre_ref[rows, :], kim_ref[rows, :]
        yre_ref[rows, :] = (zr * kr - zi * ki).astype(out)
        yim_ref[rows, :] = (zr * ki + zi * kr).astype(out)
    r = 16
    first = _first_row((r, er.shape[1]))
    e, o, p, q = er[:r], or_[:r], ei[:r], oi[:r]
    krt, krb, kit, kib = kre_ref[0:r, :], kre_ref[h:h + r, :], kim_ref[0:r, :], kim_ref[h:h + r, :]
    re4 = p * kit - q * kib
    im4 = p * kib + q * kit
    yre_ref[0:r, :] = jnp.where(first, (e + o) * krt, (e + o) * krt - (p + q) * kit).astype(out)
    yim_ref[0:r, :] = jnp.where(first, re4 + im4, (e + o) * kit + (p + q) * krt).astype(out)
    yre_ref[h:h + r, :] = jnp.where(first, (e - o) * krb, (e - o) * krb - (q - p) * kib).astype(out)
    yim_ref[h:h + r, :] = jnp.where(first, im4 - re4, (e - o) * kib + (q - p) * krb).astype(out)


def _is_natural(operand):
    return isinstance(operand[1], int)


def _dft_fwd(fwd, z, conv, kre, kim, order, tn=512):
    b = z[0].shape[0]
    c = HYENA_WIDTH
    nc = c // tn
    out = jax.ShapeDtypeStruct((b, SEQ, c), BF16)
    z_specs, z_ops = _seq_specs(z, 1, tn, lambda j, i: ((i,), j))
    in_specs = [pl.BlockSpec(fwd.shape, lambda j, i: (0, 0, 0), pipeline_mode=pl.Buffered(1))]
    in_specs += z_specs
    operands = [fwd] + z_ops
    if conv is not None:
        cw, cb, col = conv
        in_specs += [pl.BlockSpec((cw.shape[0], tn), lambda j, i: (0, col // tn + j)),
                     pl.BlockSpec((1, tn), lambda j, i: (0, col // tn + j))]
        operands += [cw, cb]
    in_specs += [pl.BlockSpec((SEQ, tn), lambda j, i: (0, order * nc + j),
                              pipeline_mode=pl.Buffered(1))] * 2
    operands += [kre, kim]
    return pl.pallas_call(
        functools.partial(_dft_fwd_body, nz=len(z_ops), natural=_is_natural(z),
                          conv=conv is not None),
        grid=(nc, b),
        in_specs=in_specs,
        out_specs=[pl.BlockSpec((None, SEQ, tn), lambda j, i: (i, 0, j))] * 2,
        out_shape=[out, out],
        compiler_params=_params("parallel", "arbitrary"),
        name="dft_fwd",
    )(*operands)


def _dft_inv_body(mats_ref, yre_ref, yim_ref, *refs, nz, z_natural, conv_z, ng, final):
    ze, zo = _load_seq(refs[:nz], z_natural)
    refs = refs[nz:]
    if conv_z:
        ze, zo = _shortconv_eo(ze, zo, *refs[:2])
        refs = refs[2:]
    ge, go = _shortconv_eo(*_load_seq(refs[:ng], True), *refs[ng:ng + 2])
    b_ref, gn_ref, *outs = refs[ng + 2:]
    h = DFT_HALF
    yrt, yrb = yre_ref[0:h, :].astype(F32), yre_ref[h:, :].astype(F32)
    yit, yib = yim_ref[0:h, :].astype(F32), yim_ref[h:, :].astype(F32)
    ye = _dot(mats_ref[0], (yrt + yrb).astype(BF16)) + _dot(mats_ref[1], (yit - yib).astype(BF16))
    yo = _dot(mats_ref[2], (yrt - yrb).astype(BF16)) + _dot(mats_ref[3], (yit + yib).astype(BF16))
    bias = b_ref[...]
    halves = (ge * (ye + bias * ze), go * (yo + bias * zo))
    if not final:
        for o_ref, zn in zip(outs, halves):
            o_ref[...] = zn
        return
    o_ref, stage_ref = outs
    for gi in range(halves[0].shape[1] // HYENA_GROUP):
        sl = slice(gi * HYENA_GROUP, (gi + 1) * HYENA_GROUP)
        for parity, zn in enumerate(halves):
            blk = zn[:, sl]
            blk = blk * lax.rsqrt(jnp.mean(blk * blk, axis=-1, keepdims=True) + RMS_EPS)
            stage_ref[pl.ds(parity, h, stride=2), :] = blk * gn_ref[:, sl]
        o_ref[:, sl] = stage_ref[...].astype(o_ref.dtype)


def _dft_inv(inv, yre, yim, z, conv_z, gate, conv_g, bias, gnorm, final, tn=512):
    b = yre.shape[0]
    c = HYENA_WIDTH
    index = lambda j, i: ((i,), j)
    taps = lambda cw, col: [pl.BlockSpec((cw.shape[0], tn), lambda j, i: (0, col // tn + j)),
                            pl.BlockSpec((1, tn), lambda j, i: (0, col // tn + j))]
    z_specs, z_ops = _seq_specs(z, 1, tn, index)
    g_specs, g_ops = _seq_specs(gate, 1, tn, index)
    in_specs = [pl.BlockSpec(inv.shape, lambda j, i: (0, 0, 0), pipeline_mode=pl.Buffered(1)),
                pl.BlockSpec((None, SEQ, tn), lambda j, i: (i, 0, j)),
                pl.BlockSpec((None, SEQ, tn), lambda j, i: (i, 0, j))] + z_specs
    operands = [inv, yre, yim] + z_ops
    if conv_z is not None:
        in_specs += taps(conv_z[0], conv_z[2])
        operands += list(conv_z[:2])
    in_specs += g_specs + taps(conv_g[0], conv_g[2])
    in_specs += [pl.BlockSpec((1, tn), lambda j, i: (0, j))] * 2
    operands += g_ops + [conv_g[0], conv_g[1], bias.reshape(1, c), gnorm.reshape(1, c)]
    if final:
        out_specs = pl.BlockSpec((None, SEQ, tn), lambda j, i: (i, 0, j))
        out_shape = jax.ShapeDtypeStruct((b, SEQ, c), BF16)
        scratch = [pltpu.VMEM((SEQ, HYENA_GROUP), F32)]
    else:
        out_specs = [pl.BlockSpec((None, DFT_HALF, tn), lambda j, i: (i, 0, j))] * 2
        out_shape = [jax.ShapeDtypeStruct((b, DFT_HALF, c), F32)] * 2
        scratch = []
    return pl.pallas_call(
        functools.partial(_dft_inv_body, nz=len(z_ops), z_natural=_is_natural(z),
                          conv_z=conv_z is not None, ng=len(g_ops), final=final),
        grid=(c // tn, b),
        in_specs=in_specs,
        out_specs=out_specs,
        out_shape=out_shape,
        scratch_shapes=scratch,
        compiler_params=_params("parallel", "arbitrary"),
        name="dft_inv",
    )(*operands)


def _hyena(proj, dft, conv_w, conv_b, w1, b1, w2, b2, freq, w3, filt_bias, g_out):
    fwd, inv = dft
    c = HYENA_WIDTH
    raw = IN_COLS - (HYENA_ORDER + 1) * c
    conv_b = conv_b.reshape(1, -1)
    filt = _hyena_filters(w1, b1, w2, b2, freq, w3)
    kre, kim = _kspec(fwd, filt)
    z, conv_z = (proj, raw), (conv_w, conv_b, 0)
    for order in range(HYENA_ORDER):
        final = order == HYENA_ORDER - 1
        yre, yim = _dft_fwd(fwd, z, conv_z, kre, kim, order)
        z = _dft_inv(inv, yre, yim, z, conv_z, (proj, raw + (order + 1) * c),
                     (conv_w, conv_b, (order + 1) * c), filt_bias[order], g_out, final)
        conv_z = None
    return z


def _rotary_tables():
    half = HEAD_DIM // 2
    pos = jnp.arange(SEQ, dtype=F32)
    inv = ROPE_THETA ** (-jnp.arange(half, dtype=F32) / half)
    ang = pos[:, None] * inv[None, :]
    cos = jnp.cos(ang)
    sin = jnp.sin(ang)
    return jnp.concatenate([cos, cos], axis=-1), jnp.concatenate([-sin, sin], axis=-1)


def _mixer(proj, dft, rot, q_norm, k_norm, conv_w, conv_b, w1, b1, w2, b2, freq, w3,
           filt_bias, attn_out_norm, hyena_out_norm):
    attn = _attention(proj, q_norm, k_norm, rot[0], rot[1], attn_out_norm)
    hy = _hyena(proj, dft, conv_w, conv_b, w1, b1, w2, b2, freq, w3, filt_bias, hyena_out_norm)
    return attn, hy


def _residual_matmul(a_list, weight, x, scale, norm_next, tm, name):
    width = weight[0].shape[-1]
    body = functools.partial(_mm_res_body, na=len(a_list), scale=scale,
                             norm_width=0 if norm_next is None else width)
    x, *normed = _matmul(body, a_list, [weight], F32, tm, 256, name, res=x, norm_next=norm_next)
    return x, normed


def _ffn(x, normed, w_gate, w_up, w_down, idx, norm_next):
    xg, rf = normed
    a, wd = _matmul(_swiglu_cast_body, [xg], [(w_gate, idx), (w_up, idx)], BF16, 2048, 256,
                    "ffn_up", row_factor=rf, cast_rows=(w_down, idx))
    return _residual_matmul([a], (wd, ()), x, 0.5, norm_next, 1024, "ffn_down")


def kernel(x, ffn_norm, ffn_w_gate, ffn_w_up, ffn_w_down, mix_norm, w_in, q_norm, k_norm, conv_w, conv_b, filt_w1, filt_b1, filt_w2, filt_b2, filt_freq, filt_w3, filt_bias, attn_out_norm, hyena_out_norm, w_out):
    b, s, d = x.shape
    x = x.reshape(b * s, d)
    dft = _dft_matrices()
    rot = _rotary_tables()
    normed = _norm_prep(x, ffn_norm[0, 0])
    for l in range(DEPTH):
        x, normed = _ffn(x, normed, ffn_w_gate, ffn_w_up, ffn_w_down, (l, 0), mix_norm[l])
        proj, = _matmul(_mm_body, [normed[0]], [(w_in, (l,))], F32, 2048, 256, "in_proj",
                        row_factor=normed[1])
        attn, hy = _mixer(proj.reshape(b, s, IN_COLS), dft, rot, q_norm[l], k_norm[l], conv_w[l],
                          conv_b[l], filt_w1[l], filt_b1[l], filt_w2[l], filt_b2[l], filt_freq[l],
                          filt_w3[l], filt_bias[l], attn_out_norm[l], hyena_out_norm[l])
        x, normed = _residual_matmul([attn.reshape(b * s, -1), hy.reshape(b * s, -1)],
                                     (w_out, (l,)), x, 1.0, ffn_norm[l, 1], 2048, "out_proj")
        last = l == DEPTH - 1
        x, normed = _ffn(x, normed, ffn_w_gate, ffn_w_up, ffn_w_down, (l, 1),
                         None if last else ffn_norm[l + 1, 0])
    return x.reshape(b, s, d)
```

```python
import functools
import math

import numpy as np
import jax
import jax.numpy as jnp
from jax import lax
from jax.experimental import pallas as pl
from jax.experimental.pallas import tpu as pltpu

D_MODEL = 4096
SEQ = 2048
DEPTH = 2
ATTN_WIDTH = D_MODEL // 2
HYENA_WIDTH = D_MODEL - ATTN_WIDTH
HEAD_DIM = 128
N_HEADS = ATTN_WIDTH // HEAD_DIM
DILATED_CONFIGS = ((128, 1), (512, 4), (2048, 16))
ROPE_THETA = 10000.0
HYENA_ORDER = 2
HYENA_GROUP = 128
FILTER_EMB_DIM = 33
DECAY_FAST = 0.3
DECAY_SLOW = 1.5
DECAY_TARGET = 1e-2
IN_COLS = 3 * ATTN_WIDTH + (HYENA_ORDER + 1) * HYENA_WIDTH
RMS_EPS = 1e-6
NEG_INF = -1e30

F32 = jnp.float32
BF16 = jnp.bfloat16

V7X_VMEM_LIMIT_BYTES = 56 * 1024 * 1024
LANES = 128
MM_TM = 2048
MM_TN = 256
FFN_DOWN_TM = 1024
NORM_TM = 512
FILT_TN = 512
KSPEC_TN = 256
DFT_TN = 512
ATTN_TQ = 256
ATTN_FAR_CONFIG = max(DILATED_CONFIGS, key=lambda wd: wd[1])
ATTN_NEAR_CONFIGS = tuple(c for c in DILATED_CONFIGS if c != ATTN_FAR_CONFIG)
ATTN_NEAR_REACH = max((w // (2 * d)) * d for w, d in ATTN_NEAR_CONFIGS)
DFT_N = 2 * SEQ
DFT_HALF = SEQ // 2
DFT_SPLIT = 32


def _params(*sem):
    return pltpu.CompilerParams(dimension_semantics=sem, vmem_limit_bytes=V7X_VMEM_LIMIT_BYTES)


def _dot(a, b):
    return jnp.dot(a, b, preferred_element_type=F32)


def _lane_group_sum(v):
    out = v[:, 0:LANES]
    for k in range(1, v.shape[1] // LANES):
        out = out + v[:, k * LANES:(k + 1) * LANES]
    return out


def _row_factor(lane_sums, width):
    ssq = jnp.sum(lane_sums, axis=-1, keepdims=True)
    return jnp.broadcast_to(lax.rsqrt(ssq / width + RMS_EPS), lane_sums.shape)


def _row_scale(rf_ref, width):
    return jnp.concatenate([rf_ref[...]] * (width // LANES), axis=1)


def _norm_prep_body(x_ref, g_ref, xg_ref, rf_ref):
    x = x_ref[...]
    xg_ref[...] = (x * g_ref[...]).astype(xg_ref.dtype)
    rf_ref[...] = _row_factor(_lane_group_sum(x * x), x.shape[1])


def _norm_prep(x, g, tm=NORM_TM):
    m, d = x.shape
    return pl.pallas_call(
        _norm_prep_body,
        grid=(m // tm,),
        in_specs=[pl.BlockSpec((tm, d), lambda i: (i, 0)),
                  pl.BlockSpec((1, d), lambda i: (0, 0))],
        out_specs=[pl.BlockSpec((tm, d), lambda i: (i, 0)),
                   pl.BlockSpec((tm, LANES), lambda i: (i, 0))],
        out_shape=[jax.ShapeDtypeStruct((m, d), BF16), jax.ShapeDtypeStruct((m, LANES), F32)],
        compiler_params=_params("parallel"),
        name="norm_prep",
    )(x, g.reshape(1, d))


def _swiglu_cast_body(a_ref, wg_ref, wu_ref, rf_ref, wd_ref, o_ref, wdb_ref):
    a = a_ref[...]
    r = _row_scale(rf_ref, o_ref.shape[1])
    g = r * _dot(a, wg_ref[...].astype(BF16))
    u = r * _dot(a, wu_ref[...].astype(BF16))
    o_ref[...] = (g * jax.nn.sigmoid(g) * u).astype(o_ref.dtype)

    @pl.when(pl.program_id(0) == 0)
    def _():
        wdb_ref[...] = wd_ref[...].astype(wdb_ref.dtype)


def _mm_body(a_ref, w_ref, rf_ref, o_ref):
    r = _row_scale(rf_ref, o_ref.shape[1])
    o_ref[...] = (r * _dot(a_ref[...], w_ref[...].astype(BF16))).astype(o_ref.dtype)


def _mm_res_body(*refs, na, scale, norm_width):
    a_refs, (w_ref, r_ref), refs = refs[:na], refs[na:na + 2], refs[na + 2:]
    acc, k0 = None, 0
    for a_ref in a_refs:
        k1 = k0 + a_ref.shape[1]
        part = _dot(a_ref[...], w_ref[k0:k1, :].astype(BF16))
        acc, k0 = part if acc is None else acc + part, k1
    x = r_ref[...] + scale * acc
    if not norm_width:
        o_ref, = refs
        o_ref[...] = x
        return
    g_ref, o_ref, xg_ref, rf_ref = refs
    o_ref[...] = x
    xg_ref[...] = (x * g_ref[...]).astype(xg_ref.dtype)
    part = _lane_group_sum(x * x)
    j = pl.program_id(1)

    @pl.when(j == 0)
    def _():
        rf_ref[...] = part

    @pl.when(j > 0)
    def _():
        rf_ref[...] += part

    @pl.when(j == pl.num_programs(1) - 1)
    def _():
        rf_ref[...] = _row_factor(rf_ref[...], norm_width)


def _matmul(body, a_list, weights, out_dtype, tm, tn, name, *, row_factor=None, res=None,
            norm_next=None, cast_rows=None):
    m = a_list[0].shape[0]
    n = weights[0][0].shape[-1]
    k = sum(a.shape[1] for a in a_list)
    nj = n // tn
    tile = pl.BlockSpec((tm, tn), lambda i, j: (i, j))
    rows = pl.BlockSpec((tm, LANES), lambda i, j: (i, 0))
    in_specs = [pl.BlockSpec((tm, a.shape[1]), lambda i, j: (i, 0), pipeline_mode=pl.Buffered(1))
                for a in a_list]
    for w, prefix in weights:
        in_specs.append(pl.BlockSpec((None,) * len(prefix) + (k, tn),
                                     lambda i, j, prefix=prefix: prefix + (0, j)))
    operands = list(a_list) + [w for w, _ in weights]
    out_specs = [tile]
    out_shape = [jax.ShapeDtypeStruct((m, n), out_dtype)]
    if row_factor is not None:
        in_specs.append(rows)
        operands.append(row_factor)
    if res is not None:
        in_specs.append(tile)
        operands.append(res)
    if norm_next is not None:
        in_specs.append(pl.BlockSpec((1, tn), lambda i, j: (0, j)))
        operands.append(norm_next.reshape(1, n))
        out_specs += [tile, rows]
        out_shape += [jax.ShapeDtypeStruct((m, n), BF16), jax.ShapeDtypeStruct((m, LANES), F32)]
    if cast_rows is not None:
        w, prefix = cast_rows
        cols = w.shape[-1]
        row_block = lambda i, j: jnp.where(i == 0, j, nj - 1)
        in_specs.append(pl.BlockSpec((None,) * len(prefix) + (tn, cols),
                                     lambda i, j: prefix + (row_block(i, j), 0)))
        operands.append(w)
        out_specs.append(pl.BlockSpec((tn, cols), lambda i, j: (row_block(i, j), 0)))
        out_shape.append(jax.ShapeDtypeStruct((n, cols), BF16))
    return pl.pallas_call(
        body,
        grid=(m // tm, nj),
        in_specs=in_specs,
        out_specs=out_specs,
        out_shape=out_shape,
        compiler_params=_params("arbitrary", "arbitrary"),
        name=name,
    )(*operands)


def _attn_near_bias():
    il = np.arange(ATTN_TQ)[:, None]
    c = np.arange(ATTN_TQ + 2 * ATTN_NEAR_REACH)[None, :]
    d = il + ATTN_NEAR_REACH - c
    mult = np.zeros(d.shape, np.int64)
    for window, dil in ATTN_NEAR_CONFIGS:
        half = window // (2 * dil)
        mult += ((d % dil) == 0) & (np.abs(d) <= half * dil)
    return np.where(mult > 0, np.log2(np.maximum(mult, 1)), NEG_INF).astype(np.float32)


def _attn_far_bias():
    window, dil = ATTN_FAR_CONFIG
    t = np.arange(SEQ // dil)
    band = np.abs(t[:, None] - t[None, :]) <= window // (2 * dil)
    return np.where(band, 0.0, NEG_INF).astype(np.float32)


def _attn_body(q_ref, k_ref, v_ref, gq_ref, gk_ref, cos_ref, sin_ref, near_ref, far_ref, go_ref,
               o_ref, qs_ref, ks_ref, vs_ref, qf_ref, kf_ref, far_acc_ref, far_den_ref, far_max_ref):
    seq = q_ref.shape[1]
    hd = HEAD_DIM
    cos = cos_ref[...]
    sin = sin_ref[...]

    def norm_rot(x, g):
        y = x * lax.rsqrt(jnp.mean(x * x, axis=-1, keepdims=True) + RMS_EPS) * g
        return y * cos + pltpu.roll(y, hd // 2, axis=1) * sin

    q_scale = math.log2(math.e) / math.sqrt(hd)
    for x_ref, g_ref, scale, f32_ref, bf16_ref in ((q_ref, gq_ref, q_scale, qf_ref, qs_ref),
                                                   (k_ref, gk_ref, 1.0, kf_ref, ks_ref)):
        x = norm_rot(x_ref[0], g_ref[...]) * scale
        f32_ref[...] = x
        bf16_ref[...] = x.astype(BF16)
    vs_ref[:, 0:hd] = v_ref[0].astype(BF16)
    vs_ref[:, hd:] = jnp.ones((seq, LANES), BF16)

    def scores(q, k, bias):
        return lax.dot_general(q, k, (((1,), (1,)), ((), ())), preferred_element_type=F32) + bias

    def probabilities(s):
        m = jnp.max(s, axis=-1, keepdims=True)
        return jnp.exp2(s - m).astype(BF16), m

    dil = ATTN_FAR_CONFIG[1]
    members = seq // dil
    classes = [pl.ds(r, members, stride=dil) for r in range(dil)]
    far_bias = far_ref[...]
    s = jnp.concatenate([scores(qf_ref[cls, :].astype(BF16), kf_ref[cls, :].astype(BF16), far_bias)
                         for cls in classes], axis=0)
    p, m = probabilities(s)
    ones = jnp.ones((members, LANES), BF16)
    for r, cls in enumerate(classes):
        block = slice(r * members, (r + 1) * members)
        v_ones = jnp.concatenate([v_ref[0, cls, :].astype(BF16), ones], axis=1)
        acc = _dot(p[block], v_ones)
        far_acc_ref[cls, :] = acc[:, 0:hd]
        far_den_ref[cls, :] = acc[:, hd:]
        far_max_ref[cls, :] = jnp.broadcast_to(m[block], (members, LANES))

    for t in range(seq // ATTN_TQ):
        q0 = t * ATTN_TQ
        rows = slice(q0, q0 + ATTN_TQ)
        lo = max(0, q0 - ATTN_NEAR_REACH)
        hi = min(seq, q0 + ATTN_TQ + ATTN_NEAR_REACH)
        c0 = lo - (q0 - ATTN_NEAR_REACH)
        p, m_near = probabilities(scores(qs_ref[rows, :], ks_ref[lo:hi, :],
                                         near_ref[:, c0:c0 + (hi - lo)]))
        acc = _dot(p, vs_ref[lo:hi, :])
        acc, den = acc[:, 0:hd], acc[:, hd:]
        m_far = far_max_ref[rows, :]
        m = jnp.maximum(m_near, m_far)
        w_near = jnp.exp2(m_near - m)
        w_far = jnp.exp2(m_far - m)
        o = ((acc * w_near + far_acc_ref[rows, :] * w_far)
             / (den * w_near + far_den_ref[rows, :] * w_far))
        o = o * lax.rsqrt(jnp.mean(o * o, axis=-1, keepdims=True) + RMS_EPS) * go_ref[...]
        o_ref[0, rows, :] = o.astype(o_ref.dtype)


def _attention(proj, gq, gk, cos, sin_signed, g_out):
    b, s, _ = proj.shape
    hd = HEAD_DIM
    near = jnp.asarray(_attn_near_bias())
    far = jnp.asarray(_attn_far_bias())
    head = lambda off: pl.BlockSpec((1, s, hd), lambda i, h: (i, 0, off + h))
    const2 = lambda shape: pl.BlockSpec(shape, lambda i, h: (0, 0))
    return pl.pallas_call(
        _attn_body,
        grid=(b, N_HEADS),
        in_specs=[head(0), head(N_HEADS), head(2 * N_HEADS),
                  const2((1, hd)), const2((1, hd)),
                  const2((s, hd)), const2((s, hd)),
                  const2(near.shape), const2(far.shape),
                  pl.BlockSpec((1, hd), lambda i, h: (0, h))],
        out_specs=pl.BlockSpec((1, s, hd), lambda i, h: (i, 0, h)),
        out_shape=jax.ShapeDtypeStruct((b, s, ATTN_WIDTH), BF16),
        scratch_shapes=[pltpu.VMEM((s, hd), BF16)] * 2 + [pltpu.VMEM((s, hd + LANES), BF16)]
        + [pltpu.VMEM((s, hd), F32)] * 5,
        compiler_params=_params("parallel", "arbitrary"),
        name="attention",
    )(proj, proj, proj, gq.reshape(1, hd), gk.reshape(1, hd), cos, sin_signed, near, far,
      g_out.reshape(1, ATTN_WIDTH))


def _shortconv_eo(ue, uo, w_ref, b_ref):
    half, edge = ue.shape[0], 8
    w0, w1, w2, b = w_ref[0:1, :], w_ref[1:2, :], w_ref[2:3, :], b_ref[...]
    taps = lambda prev, cur, nxt: w0 * prev + w1 * cur + w2 * nxt + b
    uo_prev = pltpu.roll(uo, 1, axis=0)
    ue_next = pltpu.roll(ue, half - 1, axis=0)
    row = lax.broadcasted_iota(jnp.int32, (edge, ue.shape[1]), 0)
    lo, hi = slice(0, edge), slice(half - edge, half)
    even_top = taps(jnp.where(row == 0, 0.0, uo_prev[lo]), ue[lo], uo[lo])
    odd_bot = taps(ue[hi], uo[hi], jnp.where(row == edge - 1, 0.0, ue_next[hi]))
    even = jnp.concatenate([even_top, taps(uo_prev, ue, uo)[edge:]], axis=0)
    odd = jnp.concatenate([taps(ue, uo, ue_next)[:half - edge], odd_bot], axis=0)
    return even, odd


def _filt_hidden_body(emb_ref, w1_ref, b1_ref, w2_ref, b2_ref, fr_ref, o_ref):
    hp = lax.Precision.HIGHEST
    h = jnp.dot(emb_ref[...], w1_ref[...], precision=hp, preferred_element_type=F32)
    h = jnp.sin(fr_ref[0:1, :] * (h + b1_ref[...]))
    h = jnp.dot(h, w2_ref[...], precision=hp, preferred_element_type=F32)
    o_ref[...] = jnp.sin(fr_ref[1:2, :] * (h + b2_ref[...]))


def _filt_out_body(h_ref, w3_ref, t_ref, delta_ref, o_ref):
    h = jnp.dot(h_ref[...], w3_ref[...], precision=lax.Precision.HIGHEST,
                preferred_element_type=F32)
    o_ref[...] = h * jnp.exp(-t_ref[...] * delta_ref[...])


def _pad_to(a, shape):
    return jnp.pad(a, [(0, t - s) for s, t in zip(a.shape, shape)])


def _hyena_filters(w1, b1, w2, b2, freq, w3, tn=FILT_TN):
    seq = SEQ
    t = jnp.linspace(0.0, 1.0, seq, dtype=F32)[:, None]
    bands = (FILTER_EMB_DIM - 1) // 2
    f = jnp.linspace(1e-4, bands - 1, bands, dtype=F32)[None, :]
    wpos = (2.0 * math.pi) * jnp.arange(seq, dtype=F32)[:, None] / seq
    emb = jnp.concatenate([t, jnp.cos(f * wpos), -jnp.sin(f * wpos)], axis=-1)
    deltas = jnp.abs(jnp.linspace(math.log(DECAY_FAST) / DECAY_TARGET,
                                  math.log(DECAY_SLOW) / DECAY_TARGET, HYENA_WIDTH, dtype=F32))
    ncol = w3.shape[1]
    deltas = jnp.tile(deltas, ncol // HYENA_WIDTH).reshape(1, ncol)
    hid = LANES
    hidden = pl.pallas_call(
        _filt_hidden_body,
        out_shape=jax.ShapeDtypeStruct((seq, hid), F32),
        name="filt_hidden",
    )(_pad_to(emb, (seq, hid)), _pad_to(w1, (hid, hid)), _pad_to(b1[None], (1, hid)),
      _pad_to(w2, (hid, hid)), _pad_to(b2[None], (1, hid)), _pad_to(freq, (2, hid)))
    return pl.pallas_call(
        _filt_out_body,
        grid=(ncol // tn,),
        in_specs=[pl.BlockSpec((seq, hid), lambda j: (0, 0)),
                  pl.BlockSpec((hid, tn), lambda j: (0, j)),
                  pl.BlockSpec((seq, 1), lambda j: (0, 0)),
                  pl.BlockSpec((1, tn), lambda j: (0, j))],
        out_specs=pl.BlockSpec((seq, tn), lambda j: (0, j)),
        out_shape=jax.ShapeDtypeStruct((seq, ncol), F32),
        compiler_params=_params("parallel"),
        name="filt_out",
    )(hidden, _pad_to(w3, (hid, ncol)), t, deltas)


def _dft_matrices():
    h = DFT_HALF
    def table(t, n):
        ang = ((t[:, None] * n[None, :]) & (DFT_N - 1)).astype(F32) * (2.0 * math.pi / DFT_N)
        return jnp.cos(ang), jnp.sin(ang)
    def cos_sin(n):
        c_hi, s_hi = (x[:, None, :] for x in table(jnp.arange(0, h, DFT_SPLIT, dtype=jnp.int32), n))
        c_lo, s_lo = (x[None, :, :] for x in table(jnp.arange(DFT_SPLIT, dtype=jnp.int32), n))
        return ((c_hi * c_lo - s_hi * s_lo).reshape(h, -1),
                (s_hi * c_lo + c_hi * s_lo).reshape(h, -1))
    even = 2 * jnp.arange(h, dtype=jnp.int32)
    (ce, se), (co, so) = cos_sin(even), cos_sin(even + 1)
    se, so = -se, -so
    alt = jnp.where(jnp.arange(h) % 2 == 0, 1.0, -1.0).astype(F32)
    first = jnp.arange(h) == 0
    fwd = [ce, co, jnp.where(first[:, None], alt[None, :], se),
           jnp.where(first[:, None], -alt[None, :], so)]
    wgt = jnp.where(first, 1.0 / DFT_N, 2.0 / DFT_N)[None, :]
    inv = [ce.T * wgt, jnp.where(first[None, :], alt[:, None] / DFT_N, se.T * wgt),
           co.T * wgt, jnp.where(first[None, :], -alt[:, None] / DFT_N, so.T * wgt)]
    return jnp.stack(fwd).astype(BF16), jnp.stack(inv).astype(BF16)


def _rdft(mats_ref, ze, zo):
    ze = ze.astype(BF16)
    zo = zo.astype(BF16)
    return (_dot(mats_ref[0], ze), _dot(mats_ref[1], zo), _dot(mats_ref[2], ze),
            _dot(mats_ref[3], zo))


def _first_row(shape):
    return lax.broadcasted_iota(jnp.int32, shape, 0) == 0


def _seq_specs(operand, lead, tn, index):
    def spec(rows, width, offset):
        def index_map(*g):
            pre, j = index(*g)
            return pre + (0, offset + j * (tn // width))
        return pl.BlockSpec((None,) * lead + (rows, width), index_map)
    if not _is_natural(operand):
        return [spec(DFT_HALF, tn, 0)] * 2, list(operand)
    array, col = operand
    groups = tn // LANES
    return [spec(SEQ, LANES, col // LANES + g) for g in range(groups)], [array] * groups


def _load_seq(refs, natural):
    if not natural:
        return refs[0][...], refs[1][...]
    parity = lambda p: jnp.concatenate([r[pl.ds(p, DFT_HALF, stride=2), :] for r in refs], axis=1)
    return parity(0), parity(1)


def _kspec_body(mats_ref, *refs):
    h = DFT_HALF
    *h_refs, hb0_ref, kre_ref, kim_ref = refs
    n = len(h_refs) // 2
    fer, for_, fei, foi = _rdft(mats_ref, *_load_seq(h_refs[:n], True))
    ber, bor, bei, boi = _rdft(mats_ref, *_load_seq(h_refs[n:], True))
    hb0 = hb0_ref[0:1, :]
    kre_ref[0:h, :] = (fer + for_) + (ber + bor) - hb0
    kre_ref[h:, :] = (fer - for_) + (ber - bor) - hb0
    kim_ref[0:h, :] = (fei + foi) - (bei + boi)
    kim_ref[h:, :] = (foi - fei) - (boi - bei)
    r = 8
    first = _first_row((r, fer.shape[1]))
    kim_ref[0:r, :] = jnp.where(first, fei[:r] + bei[:r] - hb0, (fei + foi - bei - boi)[:r])
    kim_ref[h:h + r, :] = jnp.where(first, foi[:r] - boi[:r], (foi - fei - boi + bei)[:r])


def _kspec(fwd, filt, tn=KSPEC_TN):
    c = HYENA_WIDTH
    nc = c // tn
    out = jax.ShapeDtypeStruct((SEQ, HYENA_ORDER * c), F32)
    index = lambda o, j: ((), o * 2 * nc + j)
    hf_specs, hf_ops = _seq_specs((filt, 0), 0, tn, index)
    hb_specs, hb_ops = _seq_specs((filt, c), 0, tn, index)
    return pl.pallas_call(
        _kspec_body,
        grid=(HYENA_ORDER, nc),
        in_specs=[pl.BlockSpec(fwd.shape, lambda o, j: (0, 0, 0), pipeline_mode=pl.Buffered(1))]
        + hf_specs + hb_specs + [pl.BlockSpec((8, tn), lambda o, j: (0, o * 2 * nc + nc + j))],
        out_specs=[pl.BlockSpec((SEQ, tn), lambda o, j: (0, o * nc + j))] * 2,
        out_shape=[out, out],
        compiler_params=_params("parallel", "arbitrary"),
        name="kspec",
    )(fwd, *hf_ops, *hb_ops, filt)


def _dft_fwd_body(mats_ref, *refs, nz, natural, conv):
    ze, zo = _load_seq(refs[:nz], natural)
    if conv:
        ze, zo = _shortconv_eo(ze, zo, *refs[nz:nz + 2])
    kre_ref, kim_ref, yre_ref, yim_ref = refs[-4:]
    h = DFT_HALF
    er, or_, ei, oi = _rdft(mats_ref, ze, zo)
    out = yre_ref.dtype
    for rows, zr, zi in ((slice(0, h), er + or_, ei + oi), (slice(h, 2 * h), er - or_, oi - ei)):
        kr, ki = kre_ref[rows, :], kim_ref[rows, :]
        yre_ref[rows, :] = (zr * kr - zi * ki).astype(out)
        yim_ref[rows, :] = (zr * ki + zi * kr).astype(out)
    r = 16
    first = _first_row((r, er.shape[1]))
    e, o, p, q = er[:r], or_[:r], ei[:r], oi[:r]
    krt, krb, kit, kib = kre_ref[0:r, :], kre_ref[h:h + r, :], kim_ref[0:r, :], kim_ref[h:h + r, :]
    re4 = p * kit - q * kib
    im4 = p * kib + q * kit
    yre_ref[0:r, :] = jnp.where(first, (e + o) * krt, (e + o) * krt - (p + q) * kit).astype(out)
    yim_ref[0:r, :] = jnp.where(first, re4 + im4, (e + o) * kit + (p + q) * krt).astype(out)
    yre_ref[h:h + r, :] = jnp.where(first, (e - o) * krb, (e - o) * krb - (q - p) * kib).astype(out)
    yim_ref[h:h + r, :] = jnp.where(first, im4 - re4, (e - o) * kib + (q - p) * krb).astype(out)


def _is_natural(operand):
    return isinstance(operand[1], int)


def _dft_fwd(fwd, z, conv, kre, kim, order, tn=DFT_TN):
    b = z[0].shape[0]
    c = HYENA_WIDTH
    nc = c // tn
    out = jax.ShapeDtypeStruct((b, SEQ, c), BF16)
    z_specs, z_ops = _seq_specs(z, 1, tn, lambda j, i: ((i,), j))
    in_specs = [pl.BlockSpec(fwd.shape, lambda j, i: (0, 0, 0), pipeline_mode=pl.Buffered(1))]
    in_specs += z_specs
    operands = [fwd] + z_ops
    if conv is not None:
        cw, cb, col = conv
        in_specs += [pl.BlockSpec((cw.shape[0], tn), lambda j, i: (0, col // tn + j)),
                     pl.BlockSpec((1, tn), lambda j, i: (0, col // tn + j))]
        operands += [cw, cb]
    in_specs += [pl.BlockSpec((SEQ, tn), lambda j, i: (0, order * nc + j))] * 2
    operands += [kre, kim]
    return pl.pallas_call(
        functools.partial(_dft_fwd_body, nz=len(z_ops), natural=_is_natural(z),
                          conv=conv is not None),
        grid=(nc, b),
        in_specs=in_specs,
        out_specs=[pl.BlockSpec((None, SEQ, tn), lambda j, i: (i, 0, j))] * 2,
        out_shape=[out, out],
        compiler_params=_params("parallel", "arbitrary"),
        name="dft_fwd",
    )(*operands)


def _dft_inv_body(mats_ref, yre_ref, yim_ref, *refs, nz, z_natural, conv_z, ng, final):
    ze, zo = _load_seq(refs[:nz], z_natural)
    refs = refs[nz:]
    if conv_z:
        ze, zo = _shortconv_eo(ze, zo, *refs[:2])
        refs = refs[2:]
    ge, go = _shortconv_eo(*_load_seq(refs[:ng], True), *refs[ng:ng + 2])
    b_ref, gn_ref, *outs = refs[ng + 2:]
    h = DFT_HALF
    yrt, yrb = yre_ref[0:h, :].astype(F32), yre_ref[h:, :].astype(F32)
    yit, yib = yim_ref[0:h, :].astype(F32), yim_ref[h:, :].astype(F32)
    ye = _dot(mats_ref[0], (yrt + yrb).astype(BF16)) + _dot(mats_ref[1], (yit - yib).astype(BF16))
    yo = _dot(mats_ref[2], (yrt - yrb).astype(BF16)) + _dot(mats_ref[3], (yit + yib).astype(BF16))
    bias = b_ref[...]
    halves = (ge * (ye + bias * ze), go * (yo + bias * zo))
    if not final:
        for o_ref, zn in zip(outs, halves):
            o_ref[...] = zn
        return
    o_ref, stage_ref = outs
    for gi in range(halves[0].shape[1] // HYENA_GROUP):
        sl = slice(gi * HYENA_GROUP, (gi + 1) * HYENA_GROUP)
        for parity, zn in enumerate(halves):
            blk = zn[:, sl]
            blk = blk * lax.rsqrt(jnp.mean(blk * blk, axis=-1, keepdims=True) + RMS_EPS)
            stage_ref[pl.ds(parity, h, stride=2), :] = blk * gn_ref[:, sl]
        o_ref[:, sl] = stage_ref[...].astype(o_ref.dtype)


def _dft_inv(inv, yre, yim, z, conv_z, gate, conv_g, bias, gnorm, final, tn=DFT_TN):
    b = yre.shape[0]
    c = HYENA_WIDTH
    index = lambda j, i: ((i,), j)
    taps = lambda cw, col: [pl.BlockSpec((cw.shape[0], tn), lambda j, i: (0, col // tn + j)),
                            pl.BlockSpec((1, tn), lambda j, i: (0, col // tn + j))]
    z_specs, z_ops = _seq_specs(z, 1, tn, index)
    g_specs, g_ops = _seq_specs(gate, 1, tn, index)
    in_specs = [pl.BlockSpec(inv.shape, lambda j, i: (0, 0, 0), pipeline_mode=pl.Buffered(1)),
                pl.BlockSpec((None, SEQ, tn), lambda j, i: (i, 0, j)),
                pl.BlockSpec((None, SEQ, tn), lambda j, i: (i, 0, j))] + z_specs
    operands = [inv, yre, yim] + z_ops
    if conv_z is not None:
        in_specs += taps(conv_z[0], conv_z[2])
        operands += list(conv_z[:2])
    in_specs += g_specs + taps(conv_g[0], conv_g[2])
    in_specs += [pl.BlockSpec((1, tn), lambda j, i: (0, j))] * 2
    operands += g_ops + [conv_g[0], conv_g[1], bias.reshape(1, c), gnorm.reshape(1, c)]
    if final:
        out_specs = pl.BlockSpec((None, SEQ, tn), lambda j, i: (i, 0, j))
        out_shape = jax.ShapeDtypeStruct((b, SEQ, c), BF16)
        scratch = [pltpu.VMEM((SEQ, HYENA_GROUP), F32)]
    else:
        out_specs = [pl.BlockSpec((None, DFT_HALF, tn), lambda j, i: (i, 0, j))] * 2
        out_shape = [jax.ShapeDtypeStruct((b, DFT_HALF, c), F32)] * 2
        scratch = []
    return pl.pallas_call(
        functools.partial(_dft_inv_body, nz=len(z_ops), z_natural=_is_natural(z),
                          conv_z=conv_z is not None, ng=len(g_ops), final=final),
        grid=(c // tn, b),
        in_specs=in_specs,
        out_specs=out_specs,
        out_shape=out_shape,
        scratch_shapes=scratch,
        compiler_params=_params("parallel", "arbitrary"),
        name="dft_inv",
    )(*operands)


def _hyena(proj, dft, conv_w, conv_b, w1, b1, w2, b2, freq, w3, filt_bias, g_out):
    fwd, inv = dft
    c = HYENA_WIDTH
    raw = IN_COLS - (HYENA_ORDER + 1) * c
    conv_b = conv_b.reshape(1, -1)
    filt = _hyena_filters(w1, b1, w2, b2, freq, w3)
    kre, kim = _kspec(fwd, filt)
    z, conv_z = (proj, raw), (conv_w, conv_b, 0)
    for order in range(HYENA_ORDER):
        final = order == HYENA_ORDER - 1
        yre, yim = _dft_fwd(fwd, z, conv_z, kre, kim, order)
        z = _dft_inv(inv, yre, yim, z, conv_z, (proj, raw + (order + 1) * c),
                     (conv_w, conv_b, (order + 1) * c), filt_bias[order], g_out, final)
        conv_z = None
    return z


def _rotary_tables():
    half = HEAD_DIM // 2
    pos = jnp.arange(SEQ, dtype=F32)
    inv = ROPE_THETA ** (-jnp.arange(half, dtype=F32) / half)
    ang = pos[:, None] * inv[None, :]
    cos = jnp.cos(ang)
    sin = jnp.sin(ang)
    return jnp.concatenate([cos, cos], axis=-1), jnp.concatenate([-sin, sin], axis=-1)


def _mixer(proj, dft, rot, q_norm, k_norm, conv_w, conv_b, w1, b1, w2, b2, freq, w3,
           filt_bias, attn_out_norm, hyena_out_norm):
    attn = _attention(proj, q_norm, k_norm, rot[0], rot[1], attn_out_norm)
    hy = _hyena(proj, dft, conv_w, conv_b, w1, b1, w2, b2, freq, w3, filt_bias, hyena_out_norm)
    return attn, hy


def _residual_matmul(a_list, weight, x, scale, norm_next, tm, name):
    width = weight[0].shape[-1]
    body = functools.partial(_mm_res_body, na=len(a_list), scale=scale,
                             norm_width=0 if norm_next is None else width)
    x, *normed = _matmul(body, a_list, [weight], F32, tm, MM_TN, name, res=x, norm_next=norm_next)
    return x, normed


def _ffn(x, normed, w_gate, w_up, w_down, idx, norm_next):
    xg, rf = normed
    a, wd = _matmul(_swiglu_cast_body, [xg], [(w_gate, idx), (w_up, idx)], BF16, MM_TM, MM_TN,
                    "ffn_up", row_factor=rf, cast_rows=(w_down, idx))
    return _residual_matmul([a], (wd, ()), x, 0.5, norm_next, FFN_DOWN_TM, "ffn_down")


def kernel(x, ffn_norm, ffn_w_gate, ffn_w_up, ffn_w_down, mix_norm, w_in, q_norm, k_norm, conv_w, conv_b, filt_w1, filt_b1, filt_w2, filt_b2, filt_freq, filt_w3, filt_bias, attn_out_norm, hyena_out_norm, w_out):
    b, s, d = x.shape
    x = x.reshape(b * s, d)
    dft = _dft_matrices()
    rot = _rotary_tables()
    normed = _norm_prep(x, ffn_norm[0, 0])
    for l in range(DEPTH):
        x, normed = _ffn(x, normed, ffn_w_gate, ffn_w_up, ffn_w_down, (l, 0), mix_norm[l])
        proj, = _matmul(_mm_body, [normed[0]], [(w_in, (l,))], F32, MM_TM, MM_TN, "in_proj",
                        row_factor=normed[1])
        attn, hy = _mixer(proj.reshape(b, s, IN_COLS), dft, rot, q_norm[l], k_norm[l], conv_w[l],
                          conv_b[l], filt_w1[l], filt_b1[l], filt_w2[l], filt_b2[l], filt_freq[l],
                          filt_w3[l], filt_bias[l], attn_out_norm[l], hyena_out_norm[l])
        x, normed = _residual_matmul([attn.reshape(b * s, -1), hy.reshape(b * s, -1)],
                                     (w_out, (l,)), x, 1.0, ffn_norm[l, 1], MM_TM, "out_proj")
        last = l == DEPTH - 1
        x, normed = _ffn(x, normed, ffn_w_gate, ffn_w_up, ffn_w_down, (l, 1),
                         None if last else ffn_norm[l + 1, 0])
    return x.reshape(b, s, d)
```

```python
import functools
import math

import numpy as np
import jax
import jax.numpy as jnp
from jax import lax
from jax.experimental import pallas as pl
from jax.experimental.pallas import tpu as pltpu

D_MODEL = 4096
SEQ = 2048
DEPTH = 2
ATTN_WIDTH = D_MODEL // 2
HYENA_WIDTH = D_MODEL - ATTN_WIDTH
HEAD_DIM = 128
N_HEADS = ATTN_WIDTH // HEAD_DIM
DILATED_CONFIGS = ((128, 1), (512, 4), (2048, 16))
ROPE_THETA = 10000.0
HYENA_ORDER = 2
HYENA_GROUP = 128
FILTER_EMB_DIM = 33
DECAY_FAST = 0.3
DECAY_SLOW = 1.5
DECAY_TARGET = 1e-2
IN_COLS = 3 * ATTN_WIDTH + (HYENA_ORDER + 1) * HYENA_WIDTH
RMS_EPS = 1e-6
NEG_INF = -1e30

F32 = jnp.float32
BF16 = jnp.bfloat16

V7X_VMEM_LIMIT_BYTES = 56 * 1024 * 1024
LANES = 128
MM_TM = 2048
MM_TN = 256
IN_PROJ_TN = 512
MM_CHUNK = 512
FFN_DOWN_TM = 1024
NORM_TM = 512
FILT_TN = 512
KSPEC_TN = 256
DFT_TN = 512
ATTN_TQ = 256
ATTN_FAR_CONFIG = max(DILATED_CONFIGS, key=lambda wd: wd[1])
ATTN_NEAR_CONFIGS = tuple(c for c in DILATED_CONFIGS if c != ATTN_FAR_CONFIG)
ATTN_NEAR_REACH = max((w // (2 * d)) * d for w, d in ATTN_NEAR_CONFIGS)
DFT_N = 2 * SEQ
DFT_HALF = SEQ // 2
DFT_SPLIT = 32


def _params(*sem):
    return pltpu.CompilerParams(dimension_semantics=sem, vmem_limit_bytes=V7X_VMEM_LIMIT_BYTES)


def _dot(a, b):
    return jnp.dot(a, b, preferred_element_type=F32)


def _lane_group_sum(v):
    out = v[:, 0:LANES]
    for k in range(1, v.shape[1] // LANES):
        out = out + v[:, k * LANES:(k + 1) * LANES]
    return out


def _row_factor(lane_sums, width):
    ssq = jnp.sum(lane_sums, axis=-1, keepdims=True)
    return jnp.broadcast_to(lax.rsqrt(ssq / width + RMS_EPS), lane_sums.shape)


def _row_scale(rf_ref, width):
    return jnp.concatenate([rf_ref[...]] * (width // LANES), axis=1)


def _norm_prep_body(x_ref, g_ref, xg_ref, rf_ref):
    x = x_ref[...]
    xg_ref[...] = (x * g_ref[...]).astype(xg_ref.dtype)
    rf_ref[...] = _row_factor(_lane_group_sum(x * x), x.shape[1])


def _norm_prep(x, g, tm=NORM_TM):
    m, d = x.shape
    return pl.pallas_call(
        _norm_prep_body,
        grid=(m // tm,),
        in_specs=[pl.BlockSpec((tm, d), lambda i: (i, 0)),
                  pl.BlockSpec((1, d), lambda i: (0, 0))],
        out_specs=[pl.BlockSpec((tm, d), lambda i: (i, 0)),
                   pl.BlockSpec((tm, LANES), lambda i: (i, 0))],
        out_shape=[jax.ShapeDtypeStruct((m, d), BF16), jax.ShapeDtypeStruct((m, LANES), F32)],
        compiler_params=_params("parallel"),
        name="norm_prep",
    )(x, g.reshape(1, d))


def _row_chunks(rows, chunk):
    return [slice(r0, r0 + chunk) for r0 in range(0, rows, chunk)]


def _swiglu_cast_body(a_ref, wg_ref, wu_ref, rf_ref, wd_ref, o_ref, wdb_ref):
    wg = wg_ref[...].astype(BF16)
    wu = wu_ref[...].astype(BF16)
    for rows in _row_chunks(a_ref.shape[0], MM_CHUNK):
        a = a_ref[rows, :]
        r = _row_scale(rf_ref.at[rows, :], o_ref.shape[1])
        g = r * _dot(a, wg)
        u = r * _dot(a, wu)
        o_ref[rows, :] = (g * jax.nn.sigmoid(g) * u).astype(o_ref.dtype)

    @pl.when(pl.program_id(0) == 0)
    def _():
        wdb_ref[...] = wd_ref[...].astype(wdb_ref.dtype)


def _mm_body(a_ref, w_ref, rf_ref, o_ref):
    w = w_ref[...].astype(BF16)
    for rows in _row_chunks(a_ref.shape[0], MM_CHUNK):
        r = _row_scale(rf_ref.at[rows, :], o_ref.shape[1])
        o_ref[rows, :] = (r * _dot(a_ref[rows, :], w)).astype(o_ref.dtype)


def _mm_res_body(*refs, na, scale, norm_width, chunk):
    a_refs, (w_ref, r_ref), refs = refs[:na], refs[na:na + 2], refs[na + 2:]
    o_ref = refs[1] if norm_width else refs[0]
    w = w_ref[...].astype(BF16)
    parts = []
    for rows in _row_chunks(o_ref.shape[0], chunk):
        acc, k0 = None, 0
        for a_ref in a_refs:
            k1 = k0 + a_ref.shape[1]
            part = _dot(a_ref[rows, :], w[k0:k1, :])
            acc, k0 = part if acc is None else acc + part, k1
        x = r_ref[rows, :] + scale * acc
        o_ref[rows, :] = x
        if norm_width:
            g_ref, _, xg_ref, rf_ref = refs
            xg_ref[rows, :] = (x * g_ref[...]).astype(xg_ref.dtype)
            parts.append(_lane_group_sum(x * x))
    if not norm_width:
        return
    part = jnp.concatenate(parts, axis=0)
    j = pl.program_id(1)

    @pl.when(j == 0)
    def _():
        rf_ref[...] = part

    @pl.when(j > 0)
    def _():
        rf_ref[...] += part

    @pl.when(j == pl.num_programs(1) - 1)
    def _():
        rf_ref[...] = _row_factor(rf_ref[...], norm_width)


def _matmul(body, a_list, weights, out_dtype, tm, tn, name, *, row_factor=None, res=None,
            norm_next=None, cast_rows=None):
    m = a_list[0].shape[0]
    n = weights[0][0].shape[-1]
    k = sum(a.shape[1] for a in a_list)
    nj = n // tn
    tile = pl.BlockSpec((tm, tn), lambda i, j: (i, j))
    rows = pl.BlockSpec((tm, LANES), lambda i, j: (i, 0))
    in_specs = [pl.BlockSpec((tm, a.shape[1]), lambda i, j: (i, 0), pipeline_mode=pl.Buffered(1))
                for a in a_list]
    for w, prefix in weights:
        in_specs.append(pl.BlockSpec((None,) * len(prefix) + (k, tn),
                                     lambda i, j, prefix=prefix: prefix + (0, j)))
    operands = list(a_list) + [w for w, _ in weights]
    out_specs = [tile]
    out_shape = [jax.ShapeDtypeStruct((m, n), out_dtype)]
    if row_factor is not None:
        in_specs.append(rows)
        operands.append(row_factor)
    if res is not None:
        in_specs.append(tile)
        operands.append(res)
    if norm_next is not None:
        in_specs.append(pl.BlockSpec((1, tn), lambda i, j: (0, j)))
        operands.append(norm_next.reshape(1, n))
        out_specs += [tile, rows]
        out_shape += [jax.ShapeDtypeStruct((m, n), BF16), jax.ShapeDtypeStruct((m, LANES), F32)]
    if cast_rows is not None:
        w, prefix = cast_rows
        cols = w.shape[-1]
        row_block = lambda i, j: jnp.where(i == 0, j, nj - 1)
        in_specs.append(pl.BlockSpec((None,) * len(prefix) + (tn, cols),
                                     lambda i, j: prefix + (row_block(i, j), 0)))
        operands.append(w)
        out_specs.append(pl.BlockSpec((tn, cols), lambda i, j: (row_block(i, j), 0)))
        out_shape.append(jax.ShapeDtypeStruct((n, cols), BF16))
    return pl.pallas_call(
        body,
        grid=(m // tm, nj),
        in_specs=in_specs,
        out_specs=out_specs,
        out_shape=out_shape,
        compiler_params=_params("arbitrary", "arbitrary"),
        name=name,
    )(*operands)


def _attn_near_bias():
    il = np.arange(ATTN_TQ)[:, None]
    c = np.arange(ATTN_TQ + 2 * ATTN_NEAR_REACH)[None, :]
    d = il + ATTN_NEAR_REACH - c
    mult = np.zeros(d.shape, np.int64)
    for window, dil in ATTN_NEAR_CONFIGS:
        half = window // (2 * dil)
        mult += ((d % dil) == 0) & (np.abs(d) <= half * dil)
    return np.where(mult > 0, np.log2(np.maximum(mult, 1)), NEG_INF).astype(np.float32)


def _attn_far_bias():
    window, dil = ATTN_FAR_CONFIG
    t = np.arange(SEQ // dil)
    band = np.abs(t[:, None] - t[None, :]) <= window // (2 * dil)
    return np.where(band, 0.0, NEG_INF).astype(np.float32)


def _attn_body(q_ref, k_ref, v_ref, gq_ref, gk_ref, cos_ref, sin_ref, near_ref, far_ref, go_ref,
               o_ref, qs_ref, ks_ref, vs_ref, qf_ref, kf_ref, far_acc_ref, far_den_ref, far_max_ref):
    seq = q_ref.shape[1]
    hd = HEAD_DIM
    cos = cos_ref[...]
    sin = sin_ref[...]

    def norm_rot(x, g):
        y = x * lax.rsqrt(jnp.mean(x * x, axis=-1, keepdims=True) + RMS_EPS) * g
        return y * cos + pltpu.roll(y, hd // 2, axis=1) * sin

    q_scale = math.log2(math.e) / math.sqrt(hd)
    for x_ref, g_ref, scale, f32_ref, bf16_ref in ((q_ref, gq_ref, q_scale, qf_ref, qs_ref),
                                                   (k_ref, gk_ref, 1.0, kf_ref, ks_ref)):
        x = norm_rot(x_ref[0], g_ref[...]) * scale
        f32_ref[...] = x
        bf16_ref[...] = x.astype(BF16)
    vs_ref[:, 0:hd] = v_ref[0].astype(BF16)
    vs_ref[:, hd:] = jnp.ones((seq, LANES), BF16)

    def scores(q, k, bias):
        return lax.dot_general(q, k, (((1,), (1,)), ((), ())), preferred_element_type=F32) + bias

    def probabilities(s):
        m = jnp.max(s, axis=-1, keepdims=True)
        return jnp.exp2(s - m).astype(BF16), m

    dil = ATTN_FAR_CONFIG[1]
    members = seq // dil
    classes = [pl.ds(r, members, stride=dil) for r in range(dil)]
    far_bias = far_ref[...]
    s = jnp.concatenate([scores(qf_ref[cls, :].astype(BF16), kf_ref[cls, :].astype(BF16), far_bias)
                         for cls in classes], axis=0)
    p, m = probabilities(s)
    ones = jnp.ones((members, LANES), BF16)
    for r, cls in enumerate(classes):
        block = slice(r * members, (r + 1) * members)
        v_ones = jnp.concatenate([v_ref[0, cls, :].astype(BF16), ones], axis=1)
        acc = _dot(p[block], v_ones)
        far_acc_ref[cls, :] = acc[:, 0:hd]
        far_den_ref[cls, :] = acc[:, hd:]
        far_max_ref[cls, :] = jnp.broadcast_to(m[block], (members, LANES))

    for t in range(seq // ATTN_TQ):
        q0 = t * ATTN_TQ
        rows = slice(q0, q0 + ATTN_TQ)
        lo = max(0, q0 - ATTN_NEAR_REACH)
        hi = min(seq, q0 + ATTN_TQ + ATTN_NEAR_REACH)
        c0 = lo - (q0 - ATTN_NEAR_REACH)
        p, m_near = probabilities(scores(qs_ref[rows, :], ks_ref[lo:hi, :],
                                         near_ref[:, c0:c0 + (hi - lo)]))
        acc = _dot(p, vs_ref[lo:hi, :])
        acc, den = acc[:, 0:hd], acc[:, hd:]
        m_far = far_max_ref[rows, :]
        m = jnp.maximum(m_near, m_far)
        w_near = jnp.exp2(m_near - m)
        w_far = jnp.exp2(m_far - m)
        o = ((acc * w_near + far_acc_ref[rows, :] * w_far)
             / (den * w_near + far_den_ref[rows, :] * w_far))
        o = o * lax.rsqrt(jnp.mean(o * o, axis=-1, keepdims=True) + RMS_EPS) * go_ref[...]
        o_ref[0, rows, :] = o.astype(o_ref.dtype)


def _attention(proj, gq, gk, cos, sin_signed, g_out):
    b, s, _ = proj.shape
    hd = HEAD_DIM
    near = jnp.asarray(_attn_near_bias())
    far = jnp.asarray(_attn_far_bias())
    head = lambda off: pl.BlockSpec((1, s, hd), lambda i, h: (i, 0, off + h))
    const2 = lambda shape: pl.BlockSpec(shape, lambda i, h: (0, 0))
    return pl.pallas_call(
        _attn_body,
        grid=(b, N_HEADS),
        in_specs=[head(0), head(N_HEADS), head(2 * N_HEADS),
                  const2((1, hd)), const2((1, hd)),
                  const2((s, hd)), const2((s, hd)),
                  const2(near.shape), const2(far.shape),
                  pl.BlockSpec((1, hd), lambda i, h: (0, h))],
        out_specs=pl.BlockSpec((1, s, hd), lambda i, h: (i, 0, h)),
        out_shape=jax.ShapeDtypeStruct((b, s, ATTN_WIDTH), BF16),
        scratch_shapes=[pltpu.VMEM((s, hd), BF16)] * 2 + [pltpu.VMEM((s, hd + LANES), BF16)]
        + [pltpu.VMEM((s, hd), F32)] * 5,
        compiler_params=_params("parallel", "arbitrary"),
        name="attention",
    )(proj, proj, proj, gq.reshape(1, hd), gk.reshape(1, hd), cos, sin_signed, near, far,
      g_out.reshape(1, ATTN_WIDTH))


def _shortconv_eo(ue, uo, w_ref, b_ref):
    half, edge = ue.shape[0], 8
    w0, w1, w2, b = w_ref[0:1, :], w_ref[1:2, :], w_ref[2:3, :], b_ref[...]
    taps = lambda prev, cur, nxt: w0 * prev + w1 * cur + w2 * nxt + b
    uo_prev = pltpu.roll(uo, 1, axis=0)
    ue_next = pltpu.roll(ue, half - 1, axis=0)
    row = lax.broadcasted_iota(jnp.int32, (edge, ue.shape[1]), 0)
    lo, hi = slice(0, edge), slice(half - edge, half)
    even_top = taps(jnp.where(row == 0, 0.0, uo_prev[lo]), ue[lo], uo[lo])
    odd_bot = taps(ue[hi], uo[hi], jnp.where(row == edge - 1, 0.0, ue_next[hi]))
    even = jnp.concatenate([even_top, taps(uo_prev, ue, uo)[edge:]], axis=0)
    odd = jnp.concatenate([taps(ue, uo, ue_next)[:half - edge], odd_bot], axis=0)
    return even, odd


def _filt_hidden_body(emb_ref, w1_ref, b1_ref, w2_ref, b2_ref, fr_ref, o_ref):
    hp = lax.Precision.HIGHEST
    h = jnp.dot(emb_ref[...], w1_ref[...], precision=hp, preferred_element_type=F32)
    h = jnp.sin(fr_ref[0:1, :] * (h + b1_ref[...]))
    h = jnp.dot(h, w2_ref[...], precision=hp, preferred_element_type=F32)
    o_ref[...] = jnp.sin(fr_ref[1:2, :] * (h + b2_ref[...]))


def _filt_out_body(h_ref, w3_ref, t_ref, delta_ref, o_ref):
    h = jnp.dot(h_ref[...], w3_ref[...], precision=lax.Precision.HIGHEST,
                preferred_element_type=F32)
    o_ref[...] = h * jnp.exp(-t_ref[...] * delta_ref[...])


def _pad_to(a, shape):
    return jnp.pad(a, [(0, t - s) for s, t in zip(a.shape, shape)])


def _hyena_filters(w1, b1, w2, b2, freq, w3, tn=FILT_TN):
    seq = SEQ
    t = jnp.linspace(0.0, 1.0, seq, dtype=F32)[:, None]
    bands = (FILTER_EMB_DIM - 1) // 2
    f = jnp.linspace(1e-4, bands - 1, bands, dtype=F32)[None, :]
    wpos = (2.0 * math.pi) * jnp.arange(seq, dtype=F32)[:, None] / seq
    emb = jnp.concatenate([t, jnp.cos(f * wpos), -jnp.sin(f * wpos)], axis=-1)
    deltas = jnp.abs(jnp.linspace(math.log(DECAY_FAST) / DECAY_TARGET,
                                  math.log(DECAY_SLOW) / DECAY_TARGET, HYENA_WIDTH, dtype=F32))
    ncol = w3.shape[1]
    deltas = jnp.tile(deltas, ncol // HYENA_WIDTH).reshape(1, ncol)
    hid = LANES
    hidden = pl.pallas_call(
        _filt_hidden_body,
        out_shape=jax.ShapeDtypeStruct((seq, hid), F32),
        name="filt_hidden",
    )(_pad_to(emb, (seq, hid)), _pad_to(w1, (hid, hid)), _pad_to(b1[None], (1, hid)),
      _pad_to(w2, (hid, hid)), _pad_to(b2[None], (1, hid)), _pad_to(freq, (2, hid)))
    return pl.pallas_call(
        _filt_out_body,
        grid=(ncol // tn,),
        in_specs=[pl.BlockSpec((seq, hid), lambda j: (0, 0)),
                  pl.BlockSpec((hid, tn), lambda j: (0, j)),
                  pl.BlockSpec((seq, 1), lambda j: (0, 0)),
                  pl.BlockSpec((1, tn), lambda j: (0, j))],
        out_specs=pl.BlockSpec((seq, tn), lambda j: (0, j)),
        out_shape=jax.ShapeDtypeStruct((seq, ncol), F32),
        compiler_params=_params("parallel"),
        name="filt_out",
    )(hidden, _pad_to(w3, (hid, ncol)), t, deltas)


def _dft_matrices():
    h = DFT_HALF
    def table(t, n):
        ang = ((t[:, None] * n[None, :]) & (DFT_N - 1)).astype(F32) * (2.0 * math.pi / DFT_N)
        return jnp.cos(ang), jnp.sin(ang)
    def cos_sin(n):
        c_hi, s_hi = (x[:, None, :] for x in table(jnp.arange(0, h, DFT_SPLIT, dtype=jnp.int32), n))
        c_lo, s_lo = (x[None, :, :] for x in table(jnp.arange(DFT_SPLIT, dtype=jnp.int32), n))
        return ((c_hi * c_lo - s_hi * s_lo).reshape(h, -1),
                (s_hi * c_lo + c_hi * s_lo).reshape(h, -1))
    even = 2 * jnp.arange(h, dtype=jnp.int32)
    (ce, se), (co, so) = cos_sin(even), cos_sin(even + 1)
    se, so = -se, -so
    alt = jnp.where(jnp.arange(h) % 2 == 0, 1.0, -1.0).astype(F32)
    first = jnp.arange(h) == 0
    fwd = [ce, co, jnp.where(first[:, None], alt[None, :], se),
           jnp.where(first[:, None], -alt[None, :], so)]
    wgt = jnp.where(first, 1.0 / DFT_N, 2.0 / DFT_N)[None, :]
    inv = [ce.T * wgt, jnp.where(first[None, :], alt[:, None] / DFT_N, se.T * wgt),
           co.T * wgt, jnp.where(first[None, :], -alt[:, None] / DFT_N, so.T * wgt)]
    return jnp.stack(fwd).astype(BF16), jnp.stack(inv).astype(BF16)


def _rdft(mats_ref, ze, zo):
    ze = ze.astype(BF16)
    zo = zo.astype(BF16)
    return (_dot(mats_ref[0], ze), _dot(mats_ref[1], zo), _dot(mats_ref[2], ze),
            _dot(mats_ref[3], zo))


def _first_row(shape):
    return lax.broadcasted_iota(jnp.int32, shape, 0) == 0


def _seq_specs(operand, lead, tn, index):
    def spec(rows, width, offset):
        def index_map(*g):
            pre, j = index(*g)
            return pre + (0, offset + j * (tn // width))
        return pl.BlockSpec((None,) * lead + (rows, width), index_map)
    if not _is_natural(operand):
        return [spec(DFT_HALF, tn, 0)] * 2, list(operand)
    array, col = operand
    groups = tn // LANES
    return [spec(SEQ, LANES, col // LANES + g) for g in range(groups)], [array] * groups


def _load_seq(refs, natural):
    if not natural:
        return refs[0][...], refs[1][...]
    parity = lambda p: jnp.concatenate([r[pl.ds(p, DFT_HALF, stride=2), :] for r in refs], axis=1)
    return parity(0), parity(1)


def _kspec_body(mats_ref, *refs):
    h = DFT_HALF
    *h_refs, hb0_ref, kre_ref, kim_ref = refs
    n = len(h_refs) // 2
    fer, for_, fei, foi = _rdft(mats_ref, *_load_seq(h_refs[:n], True))
    ber, bor, bei, boi = _rdft(mats_ref, *_load_seq(h_refs[n:], True))
    hb0 = hb0_ref[0:1, :]
    kre_ref[0:h, :] = (fer + for_) + (ber + bor) - hb0
    kre_ref[h:, :] = (fer - for_) + (ber - bor) - hb0
    kim_ref[0:h, :] = (fei + foi) - (bei + boi)
    kim_ref[h:, :] = (foi - fei) - (boi - bei)
    r = 8
    first = _first_row((r, fer.shape[1]))
    kim_ref[0:r, :] = jnp.where(first, fei[:r] + bei[:r] - hb0, (fei + foi - bei - boi)[:r])
    kim_ref[h:h + r, :] = jnp.where(first, foi[:r] - boi[:r], (foi - fei - boi + bei)[:r])


def _kspec(fwd, filt, tn=KSPEC_TN):
    c = HYENA_WIDTH
    nc = c // tn
    out = jax.ShapeDtypeStruct((SEQ, HYENA_ORDER * c), F32)
    index = lambda o, j: ((), o * 2 * nc + j)
    hf_specs, hf_ops = _seq_specs((filt, 0), 0, tn, index)
    hb_specs, hb_ops = _seq_specs((filt, c), 0, tn, index)
    return pl.pallas_call(
        _kspec_body,
        grid=(HYENA_ORDER, nc),
        in_specs=[pl.BlockSpec(fwd.shape, lambda o, j: (0, 0, 0), pipeline_mode=pl.Buffered(1))]
        + hf_specs + hb_specs + [pl.BlockSpec((8, tn), lambda o, j: (0, o * 2 * nc + nc + j))],
        out_specs=[pl.BlockSpec((SEQ, tn), lambda o, j: (0, o * nc + j))] * 2,
        out_shape=[out, out],
        compiler_params=_params("parallel", "arbitrary"),
        name="kspec",
    )(fwd, *hf_ops, *hb_ops, filt)


def _dft_fwd_body(mats_ref, *refs, nz, natural, conv):
    ze, zo = _load_seq(refs[:nz], natural)
    if conv:
        ze, zo = _shortconv_eo(ze, zo, *refs[nz:nz + 2])
    kre_ref, kim_ref, yre_ref, yim_ref = refs[-4:]
    h = DFT_HALF
    er, or_, ei, oi = _rdft(mats_ref, ze, zo)
    out = yre_ref.dtype
    for rows, zr, zi in ((slice(0, h), er + or_, ei + oi), (slice(h, 2 * h), er - or_, oi - ei)):
        kr, ki = kre_ref[rows, :], kim_ref[rows, :]
        yre_ref[rows, :] = (zr * kr - zi * ki).astype(out)
        yim_ref[rows, :] = (zr * ki + zi * kr).astype(out)
    r = 16
    first = _first_row((r, er.shape[1]))
    e, o, p, q = er[:r], or_[:r], ei[:r], oi[:r]
    krt, krb, kit, kib = kre_ref[0:r, :], kre_ref[h:h + r, :], kim_ref[0:r, :], kim_ref[h:h + r, :]
    re4 = p * kit - q * kib
    im4 = p * kib + q * kit
    yre_ref[0:r, :] = jnp.where(first, (e + o) * krt, (e + o) * krt - (p + q) * kit).astype(out)
    yim_ref[0:r, :] = jnp.where(first, re4 + im4, (e + o) * kit + (p + q) * krt).astype(out)
    yre_ref[h:h + r, :] = jnp.where(first, (e - o) * krb, (e - o) * krb - (q - p) * kib).astype(out)
    yim_ref[h:h + r, :] = jnp.where(first, im4 - re4, (e - o) * kib + (q - p) * krb).astype(out)


def _is_natural(operand):
    return isinstance(operand[1], int)


def _dft_fwd(fwd, z, conv, kre, kim, order, tn=DFT_TN):
    b = z[0].shape[0]
    c = HYENA_WIDTH
    nc = c // tn
    out = jax.ShapeDtypeStruct((b, SEQ, c), BF16)
    z_specs, z_ops = _seq_specs(z, 1, tn, lambda j, i: ((i,), j))
    in_specs = [pl.BlockSpec(fwd.shape, lambda j, i: (0, 0, 0), pipeline_mode=pl.Buffered(1))]
    in_specs += z_specs
    operands = [fwd] + z_ops
    if conv is not None:
        cw, cb, col = conv
        in_specs += [pl.BlockSpec((cw.shape[0], tn), lambda j, i: (0, col // tn + j)),
                     pl.BlockSpec((1, tn), lambda j, i: (0, col // tn + j))]
        operands += [cw, cb]
    in_specs += [pl.BlockSpec((SEQ, tn), lambda j, i: (0, order * nc + j))] * 2
    operands += [kre, kim]
    return pl.pallas_call(
        functools.partial(_dft_fwd_body, nz=len(z_ops), natural=_is_natural(z),
                          conv=conv is not None),
        grid=(nc, b),
        in_specs=in_specs,
        out_specs=[pl.BlockSpec((None, SEQ, tn), lambda j, i: (i, 0, j))] * 2,
        out_shape=[out, out],
        compiler_params=_params("parallel", "arbitrary"),
        name="dft_fwd",
    )(*operands)


def _dft_inv_body(mats_ref, yre_ref, yim_ref, *refs, nz, z_natural, conv_z, ng, final):
    ze, zo = _load_seq(refs[:nz], z_natural)
    refs = refs[nz:]
    if conv_z:
        ze, zo = _shortconv_eo(ze, zo, *refs[:2])
        refs = refs[2:]
    ge, go = _shortconv_eo(*_load_seq(refs[:ng], True), *refs[ng:ng + 2])
    b_ref, gn_ref, *outs = refs[ng + 2:]
    h = DFT_HALF
    yrt, yrb = yre_ref[0:h, :].astype(F32), yre_ref[h:, :].astype(F32)
    yit, yib = yim_ref[0:h, :].astype(F32), yim_ref[h:, :].astype(F32)
    ye = _dot(mats_ref[0], (yrt + yrb).astype(BF16)) + _dot(mats_ref[1], (yit - yib).astype(BF16))
    yo = _dot(mats_ref[2], (yrt - yrb).astype(BF16)) + _dot(mats_ref[3], (yit + yib).astype(BF16))
    bias = b_ref[...]
    halves = (ge * (ye + bias * ze), go * (yo + bias * zo))
    if not final:
        for o_ref, zn in zip(outs, halves):
            o_ref[...] = zn
        return
    o_ref, stage_ref = outs
    for gi in range(halves[0].shape[1] // HYENA_GROUP):
        sl = slice(gi * HYENA_GROUP, (gi + 1) * HYENA_GROUP)
        for parity, zn in enumerate(halves):
            blk = zn[:, sl]
            blk = blk * lax.rsqrt(jnp.mean(blk * blk, axis=-1, keepdims=True) + RMS_EPS)
            stage_ref[pl.ds(parity, h, stride=2), :] = blk * gn_ref[:, sl]
        o_ref[:, sl] = stage_ref[...].astype(o_ref.dtype)


def _dft_inv(inv, yre, yim, z, conv_z, gate, conv_g, bias, gnorm, final, tn=DFT_TN):
    b = yre.shape[0]
    c = HYENA_WIDTH
    index = lambda j, i: ((i,), j)
    taps = lambda cw, col: [pl.BlockSpec((cw.shape[0], tn), lambda j, i: (0, col // tn + j)),
                            pl.BlockSpec((1, tn), lambda j, i: (0, col // tn + j))]
    z_specs, z_ops = _seq_specs(z, 1, tn, index)
    g_specs, g_ops = _seq_specs(gate, 1, tn, index)
    in_specs = [pl.BlockSpec(inv.shape, lambda j, i: (0, 0, 0), pipeline_mode=pl.Buffered(1)),
                pl.BlockSpec((None, SEQ, tn), lambda j, i: (i, 0, j)),
                pl.BlockSpec((None, SEQ, tn), lambda j, i: (i, 0, j))] + z_specs
    operands = [inv, yre, yim] + z_ops
    if conv_z is not None:
        in_specs += taps(conv_z[0], conv_z[2])
        operands += list(conv_z[:2])
    in_specs += g_specs + taps(conv_g[0], conv_g[2])
    in_specs += [pl.BlockSpec((1, tn), lambda j, i: (0, j))] * 2
    operands += g_ops + [conv_g[0], conv_g[1], bias.reshape(1, c), gnorm.reshape(1, c)]
    if final:
        out_specs = pl.BlockSpec((None, SEQ, tn), lambda j, i: (i, 0, j))
        out_shape = jax.ShapeDtypeStruct((b, SEQ, c), BF16)
        scratch = [pltpu.VMEM((SEQ, HYENA_GROUP), F32)]
    else:
        out_specs = [pl.BlockSpec((None, DFT_HALF, tn), lambda j, i: (i, 0, j))] * 2
        out_shape = [jax.ShapeDtypeStruct((b, DFT_HALF, c), F32)] * 2
        scratch = []
    return pl.pallas_call(
        functools.partial(_dft_inv_body, nz=len(z_ops), z_natural=_is_natural(z),
                          conv_z=conv_z is not None, ng=len(g_ops), final=final),
        grid=(c // tn, b),
        in_specs=in_specs,
        out_specs=out_specs,
        out_shape=out_shape,
        scratch_shapes=scratch,
        compiler_params=_params("parallel", "arbitrary"),
        name="dft_inv",
    )(*operands)


def _hyena(proj, dft, conv_w, conv_b, w1, b1, w2, b2, freq, w3, filt_bias, g_out):
    fwd, inv = dft
    c = HYENA_WIDTH
    raw = IN_COLS - (HYENA_ORDER + 1) * c
    conv_b = conv_b.reshape(1, -1)
    filt = _hyena_filters(w1, b1, w2, b2, freq, w3)
    kre, kim = _kspec(fwd, filt)
    z, conv_z = (proj, raw), (conv_w, conv_b, 0)
    for order in range(HYENA_ORDER):
        final = order == HYENA_ORDER - 1
        yre, yim = _dft_fwd(fwd, z, conv_z, kre, kim, order)
        z = _dft_inv(inv, yre, yim, z, conv_z, (proj, raw + (order + 1) * c),
                     (conv_w, conv_b, (order + 1) * c), filt_bias[order], g_out, final)
        conv_z = None
    return z


def _rotary_tables():
    half = HEAD_DIM // 2
    pos = jnp.arange(SEQ, dtype=F32)
    inv = ROPE_THETA ** (-jnp.arange(half, dtype=F32) / half)
    ang = pos[:, None] * inv[None, :]
    cos = jnp.cos(ang)
    sin = jnp.sin(ang)
    return jnp.concatenate([cos, cos], axis=-1), jnp.concatenate([-sin, sin], axis=-1)


def _mixer(proj, dft, rot, q_norm, k_norm, conv_w, conv_b, w1, b1, w2, b2, freq, w3,
           filt_bias, attn_out_norm, hyena_out_norm):
    attn = _attention(proj, q_norm, k_norm, rot[0], rot[1], attn_out_norm)
    hy = _hyena(proj, dft, conv_w, conv_b, w1, b1, w2, b2, freq, w3, filt_bias, hyena_out_norm)
    return attn, hy


def _residual_matmul(a_list, weight, x, scale, norm_next, tm, chunk, name):
    width = weight[0].shape[-1]
    body = functools.partial(_mm_res_body, na=len(a_list), scale=scale,
                             norm_width=0 if norm_next is None else width,
                             chunk=chunk)
    x, *normed = _matmul(body, a_list, [weight], F32, tm, MM_TN, name, res=x, norm_next=norm_next)
    return x, normed


def _ffn(x, normed, w_gate, w_up, w_down, idx, norm_next):
    xg, rf = normed
    a, wd = _matmul(_swiglu_cast_body, [xg], [(w_gate, idx), (w_up, idx)], BF16, MM_TM, MM_TN,
                    "ffn_up", row_factor=rf, cast_rows=(w_down, idx))
    return _residual_matmul([a], (wd, ()), x, 0.5, norm_next, FFN_DOWN_TM, FFN_DOWN_TM, "ffn_down")


def kernel(x, ffn_norm, ffn_w_gate, ffn_w_up, ffn_w_down, mix_norm, w_in, q_norm, k_norm, conv_w, conv_b, filt_w1, filt_b1, filt_w2, filt_b2, filt_freq, filt_w3, filt_bias, attn_out_norm, hyena_out_norm, w_out):
    b, s, d = x.shape
    x = x.reshape(b * s, d)
    dft = _dft_matrices()
    rot = _rotary_tables()
    normed = _norm_prep(x, ffn_norm[0, 0])
    for l in range(DEPTH):
        x, normed = _ffn(x, normed, ffn_w_gate, ffn_w_up, ffn_w_down, (l, 0), mix_norm[l])
        proj, = _matmul(_mm_body, [normed[0]], [(w_in, (l,))], F32, MM_TM, IN_PROJ_TN, "in_proj",
                        row_factor=normed[1])
        attn, hy = _mixer(proj.reshape(b, s, IN_COLS), dft, rot, q_norm[l], k_norm[l], conv_w[l],
                          conv_b[l], filt_w1[l], filt_b1[l], filt_w2[l], filt_b2[l], filt_freq[l],
                          filt_w3[l], filt_bias[l], attn_out_norm[l], hyena_out_norm[l])
        x, normed = _residual_matmul([attn.reshape(b * s, -1), hy.reshape(b * s, -1)],
                                     (w_out, (l,)), x, 1.0, ffn_norm[l, 1], MM_TM, MM_CHUNK,
                                     "out_proj")
        last = l == DEPTH - 1
        x, normed = _ffn(x, normed, ffn_w_gate, ffn_w_up, ffn_w_down, (l, 1),
                         None if last else ffn_norm[l + 1, 0])
    return x.reshape(b, s, d)
```

```python
import functools
import math

import numpy as np
import jax
import jax.numpy as jnp
from jax import lax
from jax.experimental import pallas as pl
from jax.experimental.pallas import tpu as pltpu

D_MODEL = 4096
SEQ = 2048
DEPTH = 2
ATTN_WIDTH = D_MODEL // 2
HYENA_WIDTH = D_MODEL - ATTN_WIDTH
HEAD_DIM = 128
N_HEADS = ATTN_WIDTH // HEAD_DIM
DILATED_CONFIGS = ((128, 1), (512, 4), (2048, 16))
ROPE_THETA = 10000.0
HYENA_ORDER = 2
HYENA_GROUP = 128
FILTER_EMB_DIM = 33
DECAY_FAST = 0.3
DECAY_SLOW = 1.5
DECAY_TARGET = 1e-2
IN_COLS = 3 * ATTN_WIDTH + (HYENA_ORDER + 1) * HYENA_WIDTH
RMS_EPS = 1e-6
NEG_INF = -1e30

F32 = jnp.float32
BF16 = jnp.bfloat16

V7X_VMEM_LIMIT_BYTES = 56 * 1024 * 1024
LANES = 128
MM_TM = 2048
MM_TN = 256
IN_PROJ_TN = 512
MM_CHUNK = 256
FFN_DOWN_TM = 1024
NORM_TM = 512
FILT_TN = 512
KSPEC_TN = 256
DFT_TN = 512
ATTN_TQ = 256
ATTN_FAR_CONFIG = max(DILATED_CONFIGS, key=lambda wd: wd[1])
ATTN_NEAR_CONFIGS = tuple(c for c in DILATED_CONFIGS if c != ATTN_FAR_CONFIG)
ATTN_NEAR_REACH = max((w // (2 * d)) * d for w, d in ATTN_NEAR_CONFIGS)
DFT_N = 2 * SEQ
DFT_HALF = SEQ // 2
DFT_SPLIT = 32


def _params(*sem):
    return pltpu.CompilerParams(dimension_semantics=sem, vmem_limit_bytes=V7X_VMEM_LIMIT_BYTES)


def _dot(a, b):
    return jnp.dot(a, b, preferred_element_type=F32)


def _lane_group_sum(v):
    out = v[:, 0:LANES]
    for k in range(1, v.shape[1] // LANES):
        out = out + v[:, k * LANES:(k + 1) * LANES]
    return out


def _row_factor(lane_sums, width):
    ssq = jnp.sum(lane_sums, axis=-1, keepdims=True)
    return jnp.broadcast_to(lax.rsqrt(ssq / width + RMS_EPS), lane_sums.shape)


def _row_scale(rf_ref, width):
    return jnp.concatenate([rf_ref[...]] * (width // LANES), axis=1)


def _norm_prep_body(x_ref, g_ref, xg_ref, rf_ref):
    x = x_ref[...]
    xg_ref[...] = (x * g_ref[...]).astype(xg_ref.dtype)
    rf_ref[...] = _row_factor(_lane_group_sum(x * x), x.shape[1])


def _norm_prep(x, g, tm=NORM_TM):
    m, d = x.shape
    return pl.pallas_call(
        _norm_prep_body,
        grid=(m // tm,),
        in_specs=[pl.BlockSpec((tm, d), lambda i: (i, 0)),
                  pl.BlockSpec((1, d), lambda i: (0, 0))],
        out_specs=[pl.BlockSpec((tm, d), lambda i: (i, 0)),
                   pl.BlockSpec((tm, LANES), lambda i: (i, 0))],
        out_shape=[jax.ShapeDtypeStruct((m, d), BF16), jax.ShapeDtypeStruct((m, LANES), F32)],
        compiler_params=_params("parallel"),
        name="norm_prep",
    )(x, g.reshape(1, d))


def _row_chunks(rows, chunk):
    return [slice(r0, r0 + chunk) for r0 in range(0, rows, chunk)]


def _swiglu_cast_body(a_ref, wg_ref, wu_ref, rf_ref, wd_ref, o_ref, wdb_ref):
    wg = wg_ref[...].astype(BF16)
    wu = wu_ref[...].astype(BF16)
    for rows in _row_chunks(a_ref.shape[0], MM_CHUNK):
        a = a_ref[rows, :]
        r = _row_scale(rf_ref.at[rows, :], o_ref.shape[1])
        g = r * _dot(a, wg)
        u = r * _dot(a, wu)
        o_ref[rows, :] = (g * jax.nn.sigmoid(g) * u).astype(o_ref.dtype)

    @pl.when(pl.program_id(0) == 0)
    def _():
        wdb_ref[...] = wd_ref[...].astype(wdb_ref.dtype)


def _mm_body(a_ref, w_ref, rf_ref, o_ref):
    w = w_ref[...].astype(BF16)
    for rows in _row_chunks(a_ref.shape[0], MM_CHUNK):
        r = _row_scale(rf_ref.at[rows, :], o_ref.shape[1])
        o_ref[rows, :] = (r * _dot(a_ref[rows, :], w)).astype(o_ref.dtype)


def _mm_res_body(*refs, na, scale, norm_width, chunk):
    a_refs, (w_ref, r_ref), refs = refs[:na], refs[na:na + 2], refs[na + 2:]
    o_ref = refs[1] if norm_width else refs[0]
    w = w_ref[...].astype(BF16)
    parts = []
    for rows in _row_chunks(o_ref.shape[0], chunk):
        acc, k0 = None, 0
        for a_ref in a_refs:
            k1 = k0 + a_ref.shape[1]
            part = _dot(a_ref[rows, :], w[k0:k1, :])
            acc, k0 = part if acc is None else acc + part, k1
        x = r_ref[rows, :] + scale * acc
        o_ref[rows, :] = x
        if norm_width:
            g_ref, _, xg_ref, rf_ref = refs
            xg_ref[rows, :] = (x * g_ref[...]).astype(xg_ref.dtype)
            parts.append(_lane_group_sum(x * x))
    if not norm_width:
        return
    part = jnp.concatenate(parts, axis=0)
    j = pl.program_id(1)

    @pl.when(j == 0)
    def _():
        rf_ref[...] = part

    @pl.when(j > 0)
    def _():
        rf_ref[...] += part

    @pl.when(j == pl.num_programs(1) - 1)
    def _():
        rf_ref[...] = _row_factor(rf_ref[...], norm_width)


def _matmul(body, a_list, weights, out_dtype, tm, tn, name, *, row_factor=None, res=None,
            norm_next=None, cast_rows=None):
    m = a_list[0].shape[0]
    n = weights[0][0].shape[-1]
    k = sum(a.shape[1] for a in a_list)
    nj = n // tn
    tile = pl.BlockSpec((tm, tn), lambda i, j: (i, j))
    rows = pl.BlockSpec((tm, LANES), lambda i, j: (i, 0))
    in_specs = [pl.BlockSpec((tm, a.shape[1]), lambda i, j: (i, 0), pipeline_mode=pl.Buffered(1))
                for a in a_list]
    for w, prefix in weights:
        in_specs.append(pl.BlockSpec((None,) * len(prefix) + (k, tn),
                                     lambda i, j, prefix=prefix: prefix + (0, j)))
    operands = list(a_list) + [w for w, _ in weights]
    out_specs = [tile]
    out_shape = [jax.ShapeDtypeStruct((m, n), out_dtype)]
    if row_factor is not None:
        in_specs.append(rows)
        operands.append(row_factor)
    if res is not None:
        in_specs.append(tile)
        operands.append(res)
    if norm_next is not None:
        in_specs.append(pl.BlockSpec((1, tn), lambda i, j: (0, j)))
        operands.append(norm_next.reshape(1, n))
        out_specs += [tile, rows]
        out_shape += [jax.ShapeDtypeStruct((m, n), BF16), jax.ShapeDtypeStruct((m, LANES), F32)]
    if cast_rows is not None:
        w, prefix = cast_rows
        cols = w.shape[-1]
        row_block = lambda i, j: jnp.where(i == 0, j, nj - 1)
        in_specs.append(pl.BlockSpec((None,) * len(prefix) + (tn, cols),
                                     lambda i, j: prefix + (row_block(i, j), 0)))
        operands.append(w)
        out_specs.append(pl.BlockSpec((tn, cols), lambda i, j: (row_block(i, j), 0)))
        out_shape.append(jax.ShapeDtypeStruct((n, cols), BF16))
    return pl.pallas_call(
        body,
        grid=(m // tm, nj),
        in_specs=in_specs,
        out_specs=out_specs,
        out_shape=out_shape,
        compiler_params=_params("arbitrary", "arbitrary"),
        name=name,
    )(*operands)


def _attn_near_bias():
    il = np.arange(ATTN_TQ)[:, None]
    c = np.arange(ATTN_TQ + 2 * ATTN_NEAR_REACH)[None, :]
    d = il + ATTN_NEAR_REACH - c
    mult = np.zeros(d.shape, np.int64)
    for window, dil in ATTN_NEAR_CONFIGS:
        half = window // (2 * dil)
        mult += ((d % dil) == 0) & (np.abs(d) <= half * dil)
    return np.where(mult > 0, np.log2(np.maximum(mult, 1)), NEG_INF).astype(np.float32)


def _attn_far_bias():
    window, dil = ATTN_FAR_CONFIG
    t = np.arange(SEQ // dil)
    band = np.abs(t[:, None] - t[None, :]) <= window // (2 * dil)
    return np.where(band, 0.0, NEG_INF).astype(np.float32)


def _attn_body(q_ref, k_ref, v_ref, gq_ref, gk_ref, cos_ref, sin_ref, near_ref, far_ref, go_ref,
               o_ref, qs_ref, ks_ref, vs_ref, qf_ref, kf_ref, far_acc_ref, far_den_ref, far_max_ref):
    seq = q_ref.shape[1]
    hd = HEAD_DIM
    cos = cos_ref[...]
    sin = sin_ref[...]

    def norm_rot(x, g):
        y = x * lax.rsqrt(jnp.mean(x * x, axis=-1, keepdims=True) + RMS_EPS) * g
        return y * cos + pltpu.roll(y, hd // 2, axis=1) * sin

    q_scale = math.log2(math.e) / math.sqrt(hd)
    for x_ref, g_ref, scale, f32_ref, bf16_ref in ((q_ref, gq_ref, q_scale, qf_ref, qs_ref),
                                                   (k_ref, gk_ref, 1.0, kf_ref, ks_ref)):
        x = norm_rot(x_ref[0], g_ref[...]) * scale
        f32_ref[...] = x
        bf16_ref[...] = x.astype(BF16)
    vs_ref[:, 0:hd] = v_ref[0].astype(BF16)
    vs_ref[:, hd:] = jnp.ones((seq, LANES), BF16)

    def scores(q, k, bias):
        return lax.dot_general(q, k, (((1,), (1,)), ((), ())), preferred_element_type=F32) + bias

    def probabilities(s):
        m = jnp.max(s, axis=-1, keepdims=True)
        return jnp.exp2(s - m).astype(BF16), m

    dil = ATTN_FAR_CONFIG[1]
    members = seq // dil
    classes = [pl.ds(r, members, stride=dil) for r in range(dil)]
    far_bias = far_ref[...]
    s = jnp.concatenate([scores(qf_ref[cls, :].astype(BF16), kf_ref[cls, :].astype(BF16), far_bias)
                         for cls in classes], axis=0)
    p, m = probabilities(s)
    ones = jnp.ones((members, LANES), BF16)
    for r, cls in enumerate(classes):
        block = slice(r * members, (r + 1) * members)
        v_ones = jnp.concatenate([v_ref[0, cls, :].astype(BF16), ones], axis=1)
        acc = _dot(p[block], v_ones)
        far_acc_ref[cls, :] = acc[:, 0:hd]
        far_den_ref[cls, :] = acc[:, hd:]
        far_max_ref[cls, :] = jnp.broadcast_to(m[block], (members, LANES))

    for t in range(seq // ATTN_TQ):
        q0 = t * ATTN_TQ
        rows = slice(q0, q0 + ATTN_TQ)
        lo = max(0, q0 - ATTN_NEAR_REACH)
        hi = min(seq, q0 + ATTN_TQ + ATTN_NEAR_REACH)
        c0 = lo - (q0 - ATTN_NEAR_REACH)
        p, m_near = probabilities(scores(qs_ref[rows, :], ks_ref[lo:hi, :],
                                         near_ref[:, c0:c0 + (hi - lo)]))
        acc = _dot(p, vs_ref[lo:hi, :])
        acc, den = acc[:, 0:hd], acc[:, hd:]
        m_far = far_max_ref[rows, :]
        m = jnp.maximum(m_near, m_far)
        w_near = jnp.exp2(m_near - m)
        w_far = jnp.exp2(m_far - m)
        o = ((acc * w_near + far_acc_ref[rows, :] * w_far)
             / (den * w_near + far_den_ref[rows, :] * w_far))
        o = o * lax.rsqrt(jnp.mean(o * o, axis=-1, keepdims=True) + RMS_EPS) * go_ref[...]
        o_ref[0, rows, :] = o.astype(o_ref.dtype)


def _attention(proj, gq, gk, cos, sin_signed, g_out):
    b, s, _ = proj.shape
    hd = HEAD_DIM
    near = jnp.asarray(_attn_near_bias())
    far = jnp.asarray(_attn_far_bias())
    head = lambda off: pl.BlockSpec((1, s, hd), lambda i, h: (i, 0, off + h))
    const2 = lambda shape: pl.BlockSpec(shape, lambda i, h: (0, 0))
    return pl.pallas_call(
        _attn_body,
        grid=(b, N_HEADS),
        in_specs=[head(0), head(N_HEADS), head(2 * N_HEADS),
                  const2((1, hd)), const2((1, hd)),
                  const2((s, hd)), const2((s, hd)),
                  const2(near.shape), const2(far.shape),
                  pl.BlockSpec((1, hd), lambda i, h: (0, h))],
        out_specs=pl.BlockSpec((1, s, hd), lambda i, h: (i, 0, h)),
        out_shape=jax.ShapeDtypeStruct((b, s, ATTN_WIDTH), BF16),
        scratch_shapes=[pltpu.VMEM((s, hd), BF16)] * 2 + [pltpu.VMEM((s, hd + LANES), BF16)]
        + [pltpu.VMEM((s, hd), F32)] * 5,
        compiler_params=_params("parallel", "arbitrary"),
        name="attention",
    )(proj, proj, proj, gq.reshape(1, hd), gk.reshape(1, hd), cos, sin_signed, near, far,
      g_out.reshape(1, ATTN_WIDTH))


def _shortconv_eo(ue, uo, w_ref, b_ref):
    half, edge = ue.shape[0], 8
    w0, w1, w2, b = w_ref[0:1, :], w_ref[1:2, :], w_ref[2:3, :], b_ref[...]
    taps = lambda prev, cur, nxt: w0 * prev + w1 * cur + w2 * nxt + b
    uo_prev = pltpu.roll(uo, 1, axis=0)
    ue_next = pltpu.roll(ue, half - 1, axis=0)
    row = lax.broadcasted_iota(jnp.int32, (edge, ue.shape[1]), 0)
    lo, hi = slice(0, edge), slice(half - edge, half)
    even_top = taps(jnp.where(row == 0, 0.0, uo_prev[lo]), ue[lo], uo[lo])
    odd_bot = taps(ue[hi], uo[hi], jnp.where(row == edge - 1, 0.0, ue_next[hi]))
    even = jnp.concatenate([even_top, taps(uo_prev, ue, uo)[edge:]], axis=0)
    odd = jnp.concatenate([taps(ue, uo, ue_next)[:half - edge], odd_bot], axis=0)
    return even, odd


def _filt_hidden_body(emb_ref, w1_ref, b1_ref, w2_ref, b2_ref, fr_ref, o_ref):
    hp = lax.Precision.HIGHEST
    h = jnp.dot(emb_ref[...], w1_ref[...], precision=hp, preferred_element_type=F32)
    h = jnp.sin(fr_ref[0:1, :] * (h + b1_ref[...]))
    h = jnp.dot(h, w2_ref[...], precision=hp, preferred_element_type=F32)
    o_ref[...] = jnp.sin(fr_ref[1:2, :] * (h + b2_ref[...]))


def _split_bf16(x):
    high = x.astype(BF16)
    return high, (x - high.astype(F32)).astype(BF16)


def _filt_out_body(h_ref, w3_ref, t_ref, delta_ref, o_ref):
    h_hi, h_lo = _split_bf16(h_ref[...])
    w_hi, w_lo = _split_bf16(w3_ref[...])
    h = _dot(h_hi, w_hi) + (_dot(h_hi, w_lo) + _dot(h_lo, w_hi))
    o_ref[...] = h * jnp.exp(-t_ref[...] * delta_ref[...])


def _pad_to(a, shape):
    return jnp.pad(a, [(0, t - s) for s, t in zip(a.shape, shape)])


def _hyena_filters(w1, b1, w2, b2, freq, w3, tn=FILT_TN):
    seq = SEQ
    t = jnp.linspace(0.0, 1.0, seq, dtype=F32)[:, None]
    bands = (FILTER_EMB_DIM - 1) // 2
    f = jnp.linspace(1e-4, bands - 1, bands, dtype=F32)[None, :]
    wpos = (2.0 * math.pi) * jnp.arange(seq, dtype=F32)[:, None] / seq
    emb = jnp.concatenate([t, jnp.cos(f * wpos), -jnp.sin(f * wpos)], axis=-1)
    deltas = jnp.abs(jnp.linspace(math.log(DECAY_FAST) / DECAY_TARGET,
                                  math.log(DECAY_SLOW) / DECAY_TARGET, HYENA_WIDTH, dtype=F32))
    ncol = w3.shape[1]
    deltas = jnp.tile(deltas, ncol // HYENA_WIDTH).reshape(1, ncol)
    hid = LANES
    hidden = pl.pallas_call(
        _filt_hidden_body,
        out_shape=jax.ShapeDtypeStruct((seq, hid), F32),
        name="filt_hidden",
    )(_pad_to(emb, (seq, hid)), _pad_to(w1, (hid, hid)), _pad_to(b1[None], (1, hid)),
      _pad_to(w2, (hid, hid)), _pad_to(b2[None], (1, hid)), _pad_to(freq, (2, hid)))
    return pl.pallas_call(
        _filt_out_body,
        grid=(ncol // tn,),
        in_specs=[pl.BlockSpec((seq, hid), lambda j: (0, 0)),
                  pl.BlockSpec((hid, tn), lambda j: (0, j)),
                  pl.BlockSpec((seq, 1), lambda j: (0, 0)),
                  pl.BlockSpec((1, tn), lambda j: (0, j))],
        out_specs=pl.BlockSpec((seq, tn), lambda j: (0, j)),
        out_shape=jax.ShapeDtypeStruct((seq, ncol), F32),
        compiler_params=_params("parallel"),
        name="filt_out",
    )(hidden, _pad_to(w3, (hid, ncol)), t, deltas)


def _dft_matrices():
    h = DFT_HALF
    def table(t, n):
        ang = ((t[:, None] * n[None, :]) & (DFT_N - 1)).astype(F32) * (2.0 * math.pi / DFT_N)
        return jnp.cos(ang), jnp.sin(ang)
    def cos_sin(n):
        c_hi, s_hi = (x[:, None, :] for x in table(jnp.arange(0, h, DFT_SPLIT, dtype=jnp.int32), n))
        c_lo, s_lo = (x[None, :, :] for x in table(jnp.arange(DFT_SPLIT, dtype=jnp.int32), n))
        return ((c_hi * c_lo - s_hi * s_lo).reshape(h, -1),
                (s_hi * c_lo + c_hi * s_lo).reshape(h, -1))
    even = 2 * jnp.arange(h, dtype=jnp.int32)
    (ce, se), (co, so) = cos_sin(even), cos_sin(even + 1)
    se, so = -se, -so
    alt = jnp.where(jnp.arange(h) % 2 == 0, 1.0, -1.0).astype(F32)
    first = jnp.arange(h) == 0
    fwd = [ce, co, jnp.where(first[:, None], alt[None, :], se),
           jnp.where(first[:, None], -alt[None, :], so)]
    wgt = jnp.where(first, 1.0 / DFT_N, 2.0 / DFT_N)[None, :]
    inv = [ce.T * wgt, jnp.where(first[None, :], alt[:, None] / DFT_N, se.T * wgt),
           co.T * wgt, jnp.where(first[None, :], -alt[:, None] / DFT_N, so.T * wgt)]
    return jnp.stack(fwd).astype(BF16), jnp.stack(inv).astype(BF16)


def _rdft(mats_ref, ze, zo):
    ze = ze.astype(BF16)
    zo = zo.astype(BF16)
    return (_dot(mats_ref[0], ze), _dot(mats_ref[1], zo), _dot(mats_ref[2], ze),
            _dot(mats_ref[3], zo))


def _first_row(shape):
    return lax.broadcasted_iota(jnp.int32, shape, 0) == 0


def _seq_specs(operand, lead, tn, index):
    def spec(rows, width, offset):
        def index_map(*g):
            pre, j = index(*g)
            return pre + (0, offset + j * (tn // width))
        return pl.BlockSpec((None,) * lead + (rows, width), index_map)
    if not _is_natural(operand):
        return [spec(DFT_HALF, tn, 0)] * 2, list(operand)
    array, col = operand
    groups = tn // LANES
    return [spec(SEQ, LANES, col // LANES + g) for g in range(groups)], [array] * groups


def _load_seq(refs, natural):
    if not natural:
        return refs[0][...], refs[1][...]
    parity = lambda p: jnp.concatenate([r[pl.ds(p, DFT_HALF, stride=2), :] for r in refs], axis=1)
    return parity(0), parity(1)


def _kspec_body(mats_ref, *refs):
    h = DFT_HALF
    *h_refs, hb0_ref, kre_ref, kim_ref = refs
    n = len(h_refs) // 2
    fer, for_, fei, foi = _rdft(mats_ref, *_load_seq(h_refs[:n], True))
    ber, bor, bei, boi = _rdft(mats_ref, *_load_seq(h_refs[n:], True))
    hb0 = hb0_ref[0:1, :]
    kre_ref[0:h, :] = (fer + for_) + (ber + bor) - hb0
    kre_ref[h:, :] = (fer - for_) + (ber - bor) - hb0
    kim_ref[0:h, :] = (fei + foi) - (bei + boi)
    kim_ref[h:, :] = (foi - fei) - (boi - bei)
    r = 8
    first = _first_row((r, fer.shape[1]))
    kim_ref[0:r, :] = jnp.where(first, fei[:r] + bei[:r] - hb0, (fei + foi - bei - boi)[:r])
    kim_ref[h:h + r, :] = jnp.where(first, foi[:r] - boi[:r], (foi - fei - boi + bei)[:r])


def _kspec(fwd, filt, tn=KSPEC_TN):
    c = HYENA_WIDTH
    nc = c // tn
    out = jax.ShapeDtypeStruct((SEQ, HYENA_ORDER * c), F32)
    index = lambda o, j: ((), o * 2 * nc + j)
    hf_specs, hf_ops = _seq_specs((filt, 0), 0, tn, index)
    hb_specs, hb_ops = _seq_specs((filt, c), 0, tn, index)
    return pl.pallas_call(
        _kspec_body,
        grid=(HYENA_ORDER, nc),
        in_specs=[pl.BlockSpec(fwd.shape, lambda o, j: (0, 0, 0), pipeline_mode=pl.Buffered(1))]
        + hf_specs + hb_specs + [pl.BlockSpec((8, tn), lambda o, j: (0, o * 2 * nc + nc + j))],
        out_specs=[pl.BlockSpec((SEQ, tn), lambda o, j: (0, o * nc + j))] * 2,
        out_shape=[out, out],
        compiler_params=_params("parallel", "arbitrary"),
        name="kspec",
    )(fwd, *hf_ops, *hb_ops, filt)


def _dft_fwd_body(mats_ref, *refs, nz, natural, conv):
    ze, zo = _load_seq(refs[:nz], natural)
    if conv:
        ze, zo = _shortconv_eo(ze, zo, *refs[nz:nz + 2])
    kre_ref, kim_ref, yre_ref, yim_ref = refs[-4:]
    h = DFT_HALF
    er, or_, ei, oi = _rdft(mats_ref, ze, zo)
    out = yre_ref.dtype
    for rows, zr, zi in ((slice(0, h), er + or_, ei + oi), (slice(h, 2 * h), er - or_, oi - ei)):
        kr, ki = kre_ref[rows, :], kim_ref[rows, :]
        yre_ref[rows, :] = (zr * kr - zi * ki).astype(out)
        yim_ref[rows, :] = (zr * ki + zi * kr).astype(out)
    r = 16
    first = _first_row((r, er.shape[1]))
    e, o, p, q = er[:r], or_[:r], ei[:r], oi[:r]
    krt, krb, kit, kib = kre_ref[0:r, :], kre_ref[h:h + r, :], kim_ref[0:r, :], kim_ref[h:h + r, :]
    re4 = p * kit - q * kib
    im4 = p * kib + q * kit
    yre_ref[0:r, :] = jnp.where(first, (e + o) * krt, (e + o) * krt - (p + q) * kit).astype(out)
    yim_ref[0:r, :] = jnp.where(first, re4 + im4, (e + o) * kit + (p + q) * krt).astype(out)
    yre_ref[h:h + r, :] = jnp.where(first, (e - o) * krb, (e - o) * krb - (q - p) * kib).astype(out)
    yim_ref[h:h + r, :] = jnp.where(first, im4 - re4, (e - o) * kib + (q - p) * krb).astype(out)


def _is_natural(operand):
    return isinstance(operand[1], int)


def _dft_fwd(fwd, z, conv, kre, kim, order, tn=DFT_TN):
    b = z[0].shape[0]
    c = HYENA_WIDTH
    nc = c // tn
    out = jax.ShapeDtypeStruct((b, SEQ, c), BF16)
    z_specs, z_ops = _seq_specs(z, 1, tn, lambda j, i: ((i,), j))
    in_specs = [pl.BlockSpec(fwd.shape, lambda j, i: (0, 0, 0), pipeline_mode=pl.Buffered(1))]
    in_specs += z_specs
    operands = [fwd] + z_ops
    if conv is not None:
        cw, cb, col = conv
        in_specs += [pl.BlockSpec((cw.shape[0], tn), lambda j, i: (0, col // tn + j)),
                     pl.BlockSpec((1, tn), lambda j, i: (0, col // tn + j))]
        operands += [cw, cb]
    in_specs += [pl.BlockSpec((SEQ, tn), lambda j, i: (0, order * nc + j))] * 2
    operands += [kre, kim]
    return pl.pallas_call(
        functools.partial(_dft_fwd_body, nz=len(z_ops), natural=_is_natural(z),
                          conv=conv is not None),
        grid=(nc, b),
        in_specs=in_specs,
        out_specs=[pl.BlockSpec((None, SEQ, tn), lambda j, i: (i, 0, j))] * 2,
        out_shape=[out, out],
        compiler_params=_params("parallel", "arbitrary"),
        name="dft_fwd",
    )(*operands)


def _dft_inv_body(mats_ref, yre_ref, yim_ref, *refs, nz, z_natural, conv_z, ng, final):
    ze, zo = _load_seq(refs[:nz], z_natural)
    refs = refs[nz:]
    if conv_z:
        ze, zo = _shortconv_eo(ze, zo, *refs[:2])
        refs = refs[2:]
    ge, go = _shortconv_eo(*_load_seq(refs[:ng], True), *refs[ng:ng + 2])
    b_ref, gn_ref, *outs = refs[ng + 2:]
    h = DFT_HALF
    yrt, yrb = yre_ref[0:h, :].astype(F32), yre_ref[h:, :].astype(F32)
    yit, yib = yim_ref[0:h, :].astype(F32), yim_ref[h:, :].astype(F32)
    ye = _dot(mats_ref[0], (yrt + yrb).astype(BF16)) + _dot(mats_ref[1], (yit - yib).astype(BF16))
    yo = _dot(mats_ref[2], (yrt - yrb).astype(BF16)) + _dot(mats_ref[3], (yit + yib).astype(BF16))
    bias = b_ref[...]
    halves = (ge * (ye + bias * ze), go * (yo + bias * zo))
    if not final:
        for o_ref, zn in zip(outs, halves):
            o_ref[...] = zn
        return
    o_ref, stage_ref = outs
    for gi in range(halves[0].shape[1] // HYENA_GROUP):
        sl = slice(gi * HYENA_GROUP, (gi + 1) * HYENA_GROUP)
        for parity, zn in enumerate(halves):
            blk = zn[:, sl]
            blk = blk * lax.rsqrt(jnp.mean(blk * blk, axis=-1, keepdims=True) + RMS_EPS)
            stage_ref[pl.ds(parity, h, stride=2), :] = blk * gn_ref[:, sl]
        o_ref[:, sl] = stage_ref[...].astype(o_ref.dtype)


def _dft_inv(inv, yre, yim, z, conv_z, gate, conv_g, bias, gnorm, final, tn=DFT_TN):
    b = yre.shape[0]
    c = HYENA_WIDTH
    index = lambda j, i: ((i,), j)
    taps = lambda cw, col: [pl.BlockSpec((cw.shape[0], tn), lambda j, i: (0, col // tn + j)),
                            pl.BlockSpec((1, tn), lambda j, i: (0, col // tn + j))]
    z_specs, z_ops = _seq_specs(z, 1, tn, index)
    g_specs, g_ops = _seq_specs(gate, 1, tn, index)
    in_specs = [pl.BlockSpec(inv.shape, lambda j, i: (0, 0, 0), pipeline_mode=pl.Buffered(1)),
                pl.BlockSpec((None, SEQ, tn), lambda j, i: (i, 0, j)),
                pl.BlockSpec((None, SEQ, tn), lambda j, i: (i, 0, j))] + z_specs
    operands = [inv, yre, yim] + z_ops
    if conv_z is not None:
        in_specs += taps(conv_z[0], conv_z[2])
        operands += list(conv_z[:2])
    in_specs += g_specs + taps(conv_g[0], conv_g[2])
    in_specs += [pl.BlockSpec((1, tn), lambda j, i: (0, j))] * 2
    operands += g_ops + [conv_g[0], conv_g[1], bias.reshape(1, c), gnorm.reshape(1, c)]
    if final:
        out_specs = pl.BlockSpec((None, SEQ, tn), lambda j, i: (i, 0, j))
        out_shape = jax.ShapeDtypeStruct((b, SEQ, c), BF16)
        scratch = [pltpu.VMEM((SEQ, HYENA_GROUP), F32)]
    else:
        out_specs = [pl.BlockSpec((None, DFT_HALF, tn), lambda j, i: (i, 0, j))] * 2
        out_shape = [jax.ShapeDtypeStruct((b, DFT_HALF, c), F32)] * 2
        scratch = []
    return pl.pallas_call(
        functools.partial(_dft_inv_body, nz=len(z_ops), z_natural=_is_natural(z),
                          conv_z=conv_z is not None, ng=len(g_ops), final=final),
        grid=(c // tn, b),
        in_specs=in_specs,
        out_specs=out_specs,
        out_shape=out_shape,
        scratch_shapes=scratch,
        compiler_params=_params("parallel", "arbitrary"),
        name="dft_inv",
    )(*operands)


def _hyena(proj, dft, conv_w, conv_b, w1, b1, w2, b2, freq, w3, filt_bias, g_out):
    fwd, inv = dft
    c = HYENA_WIDTH
    raw = IN_COLS - (HYENA_ORDER + 1) * c
    conv_b = conv_b.reshape(1, -1)
    filt = _hyena_filters(w1, b1, w2, b2, freq, w3)
    kre, kim = _kspec(fwd, filt)
    z, conv_z = (proj, raw), (conv_w, conv_b, 0)
    for order in range(HYENA_ORDER):
        final = order == HYENA_ORDER - 1
        yre, yim = _dft_fwd(fwd, z, conv_z, kre, kim, order)
        z = _dft_inv(inv, yre, yim, z, conv_z, (proj, raw + (order + 1) * c),
                     (conv_w, conv_b, (order + 1) * c), filt_bias[order], g_out, final)
        conv_z = None
    return z


def _rotary_tables():
    half = HEAD_DIM // 2
    pos = jnp.arange(SEQ, dtype=F32)
    inv = ROPE_THETA ** (-jnp.arange(half, dtype=F32) / half)
    ang = pos[:, None] * inv[None, :]
    cos = jnp.cos(ang)
    sin = jnp.sin(ang)
    return jnp.concatenate([cos, cos], axis=-1), jnp.concatenate([-sin, sin], axis=-1)


def _mixer(proj, dft, rot, q_norm, k_norm, conv_w, conv_b, w1, b1, w2, b2, freq, w3,
           filt_bias, attn_out_norm, hyena_out_norm):
    attn = _attention(proj, q_norm, k_norm, rot[0], rot[1], attn_out_norm)
    hy = _hyena(proj, dft, conv_w, conv_b, w1, b1, w2, b2, freq, w3, filt_bias, hyena_out_norm)
    return attn, hy


def _residual_matmul(a_list, weight, x, scale, norm_next, tm, chunk, name):
    width = weight[0].shape[-1]
    body = functools.partial(_mm_res_body, na=len(a_list), scale=scale,
                             norm_width=0 if norm_next is None else width,
                             chunk=chunk)
    x, *normed = _matmul(body, a_list, [weight], F32, tm, MM_TN, name, res=x, norm_next=norm_next)
    return x, normed


def _ffn(x, normed, w_gate, w_up, w_down, idx, norm_next):
    xg, rf = normed
    a, wd = _matmul(_swiglu_cast_body, [xg], [(w_gate, idx), (w_up, idx)], BF16, MM_TM, MM_TN,
                    "ffn_up", row_factor=rf, cast_rows=(w_down, idx))
    return _residual_matmul([a], (wd, ()), x, 0.5, norm_next, FFN_DOWN_TM, FFN_DOWN_TM, "ffn_down")


def kernel(x, ffn_norm, ffn_w_gate, ffn_w_up, ffn_w_down, mix_norm, w_in, q_norm, k_norm, conv_w, conv_b, filt_w1, filt_b1, filt_w2, filt_b2, filt_freq, filt_w3, filt_bias, attn_out_norm, hyena_out_norm, w_out):
    b, s, d = x.shape
    x = x.reshape(b * s, d)
    dft = _dft_matrices()
    rot = _rotary_tables()
    normed = _norm_prep(x, ffn_norm[0, 0])
    for l in range(DEPTH):
        x, normed = _ffn(x, normed, ffn_w_gate, ffn_w_up, ffn_w_down, (l, 0), mix_norm[l])
        proj, = _matmul(_mm_body, [normed[0]], [(w_in, (l,))], F32, MM_TM, IN_PROJ_TN, "in_proj",
                        row_factor=normed[1])
        attn, hy = _mixer(proj.reshape(b, s, IN_COLS), dft, rot, q_norm[l], k_norm[l], conv_w[l],
                          conv_b[l], filt_w1[l], filt_b1[l], filt_w2[l], filt_b2[l], filt_freq[l],
                          filt_w3[l], filt_bias[l], attn_out_norm[l], hyena_out_norm[l])
        x, normed = _residual_matmul([attn.reshape(b * s, -1), hy.reshape(b * s, -1)],
                                     (w_out, (l,)), x, 1.0, ffn_norm[l, 1], MM_TM, MM_CHUNK,
                                     "out_proj")
        last = l == DEPTH - 1
        x, normed = _ffn(x, normed, ffn_w_gate, ffn_w_up, ffn_w_down, (l, 1),
                         None if last else ffn_norm[l + 1, 0])
    return x.reshape(b, s, d)
```

```python
import functools
import math

import numpy as np
import jax
import jax.numpy as jnp
from jax import lax
from jax.experimental import pallas as pl
from jax.experimental.pallas import tpu as pltpu

D_MODEL = 4096
SEQ = 2048
DEPTH = 2
ATTN_WIDTH = D_MODEL // 2
HYENA_WIDTH = D_MODEL - ATTN_WIDTH
HEAD_DIM = 128
N_HEADS = ATTN_WIDTH // HEAD_DIM
DILATED_CONFIGS = ((128, 1), (512, 4), (2048, 16))
ROPE_THETA = 10000.0
HYENA_ORDER = 2
HYENA_GROUP = 128
FILTER_EMB_DIM = 33
DECAY_FAST = 0.3
DECAY_SLOW = 1.5
DECAY_TARGET = 1e-2
IN_COLS = 3 * ATTN_WIDTH + (HYENA_ORDER + 1) * HYENA_WIDTH
RMS_EPS = 1e-6
NEG_INF = -1e30

F32 = jnp.float32
BF16 = jnp.bfloat16

V7X_VMEM_LIMIT_BYTES = 56 * 1024 * 1024
LANES = 128
MM_TM = 2048
MM_TN = 256
IN_PROJ_TN = 512
OUT_PROJ_TM = 1024
OUT_PROJ_TN = 512
MM_CHUNK = 256
FFN_DOWN_TM = 1024
NORM_TM = 512
FILT_TN = 512
KSPEC_TN = 256
DFT_TN = 512
ATTN_TQ = 256
ATTN_FAR_CONFIG = max(DILATED_CONFIGS, key=lambda wd: wd[1])
ATTN_NEAR_CONFIGS = tuple(c for c in DILATED_CONFIGS if c != ATTN_FAR_CONFIG)
ATTN_NEAR_REACH = max((w // (2 * d)) * d for w, d in ATTN_NEAR_CONFIGS)
DFT_N = 2 * SEQ
DFT_HALF = SEQ // 2
DFT_SPLIT = 32


def _params(*sem):
    return pltpu.CompilerParams(dimension_semantics=sem, vmem_limit_bytes=V7X_VMEM_LIMIT_BYTES)


def _dot(a, b):
    return jnp.dot(a, b, preferred_element_type=F32)


def _lane_group_sum(v):
    out = v[:, 0:LANES]
    for k in range(1, v.shape[1] // LANES):
        out = out + v[:, k * LANES:(k + 1) * LANES]
    return out


def _row_factor(lane_sums, width):
    ssq = jnp.sum(lane_sums, axis=-1, keepdims=True)
    return jnp.broadcast_to(lax.rsqrt(ssq / width + RMS_EPS), lane_sums.shape)


def _row_scale(rf_ref, width):
    return jnp.concatenate([rf_ref[...]] * (width // LANES), axis=1)


def _norm_prep_body(x_ref, g_ref, xg_ref, rf_ref):
    x = x_ref[...]
    xg_ref[...] = (x * g_ref[...]).astype(xg_ref.dtype)
    rf_ref[...] = _row_factor(_lane_group_sum(x * x), x.shape[1])


def _norm_prep(x, g, tm=NORM_TM):
    m, d = x.shape
    return pl.pallas_call(
        _norm_prep_body,
        grid=(m // tm,),
        in_specs=[pl.BlockSpec((tm, d), lambda i: (i, 0)),
                  pl.BlockSpec((1, d), lambda i: (0, 0))],
        out_specs=[pl.BlockSpec((tm, d), lambda i: (i, 0)),
                   pl.BlockSpec((tm, LANES), lambda i: (i, 0))],
        out_shape=[jax.ShapeDtypeStruct((m, d), BF16), jax.ShapeDtypeStruct((m, LANES), F32)],
        compiler_params=_params("parallel"),
        name="norm_prep",
    )(x, g.reshape(1, d))


def _row_chunks(rows, chunk):
    return [slice(r0, r0 + chunk) for r0 in range(0, rows, chunk)]


def _swiglu_cast_body(a_ref, wg_ref, wu_ref, rf_ref, wd_ref, o_ref, wdb_ref):
    wg = wg_ref[...].astype(BF16)
    wu = wu_ref[...].astype(BF16)
    for rows in _row_chunks(a_ref.shape[0], MM_CHUNK):
        a = a_ref[rows, :]
        r = _row_scale(rf_ref.at[rows, :], o_ref.shape[1])
        g = r * _dot(a, wg)
        u = r * _dot(a, wu)
        o_ref[rows, :] = (g * jax.nn.sigmoid(g) * u).astype(o_ref.dtype)

    @pl.when(pl.program_id(0) == 0)
    def _():
        wdb_ref[...] = wd_ref[...].astype(wdb_ref.dtype)


def _mm_body(a_ref, w_ref, rf_ref, o_ref):
    w = w_ref[...].astype(BF16)
    for rows in _row_chunks(a_ref.shape[0], MM_CHUNK):
        r = _row_scale(rf_ref.at[rows, :], o_ref.shape[1])
        o_ref[rows, :] = (r * _dot(a_ref[rows, :], w)).astype(o_ref.dtype)


def _mm_res_body(*refs, na, scale, norm_width, chunk):
    a_refs, (w_ref, r_ref), refs = refs[:na], refs[na:na + 2], refs[na + 2:]
    o_ref = refs[1] if norm_width else refs[0]
    w = w_ref[...].astype(BF16)
    parts = []
    for rows in _row_chunks(o_ref.shape[0], chunk):
        acc, k0 = None, 0
        for a_ref in a_refs:
            k1 = k0 + a_ref.shape[1]
            part = _dot(a_ref[rows, :], w[k0:k1, :])
            acc, k0 = part if acc is None else acc + part, k1
        x = r_ref[rows, :] + scale * acc
        o_ref[rows, :] = x
        if norm_width:
            g_ref, _, xg_ref, rf_ref = refs
            xg_ref[rows, :] = (x * g_ref[...]).astype(xg_ref.dtype)
            parts.append(_lane_group_sum(x * x))
    if not norm_width:
        return
    part = jnp.concatenate(parts, axis=0)
    j = pl.program_id(1)

    @pl.when(j == 0)
    def _():
        rf_ref[...] = part

    @pl.when(j > 0)
    def _():
        rf_ref[...] += part

    @pl.when(j == pl.num_programs(1) - 1)
    def _():
        rf_ref[...] = _row_factor(rf_ref[...], norm_width)


def _matmul(body, a_list, weights, out_dtype, tm, tn, name, *, row_factor=None, res=None,
            norm_next=None, cast_rows=None, a_buffers=1):
    m = a_list[0].shape[0]
    n = weights[0][0].shape[-1]
    k = sum(a.shape[1] for a in a_list)
    nj = n // tn
    tile = pl.BlockSpec((tm, tn), lambda i, j: (i, j))
    rows = pl.BlockSpec((tm, LANES), lambda i, j: (i, 0))
    in_specs = [pl.BlockSpec((tm, a.shape[1]), lambda i, j: (i, 0),
                             pipeline_mode=pl.Buffered(a_buffers)) for a in a_list]
    for w, prefix in weights:
        in_specs.append(pl.BlockSpec((None,) * len(prefix) + (k, tn),
                                     lambda i, j, prefix=prefix: prefix + (0, j)))
    operands = list(a_list) + [w for w, _ in weights]
    out_specs = [tile]
    out_shape = [jax.ShapeDtypeStruct((m, n), out_dtype)]
    if row_factor is not None:
        in_specs.append(rows)
        operands.append(row_factor)
    if res is not None:
        in_specs.append(tile)
        operands.append(res)
    if norm_next is not None:
        in_specs.append(pl.BlockSpec((1, tn), lambda i, j: (0, j)))
        operands.append(norm_next.reshape(1, n))
        out_specs += [tile, rows]
        out_shape += [jax.ShapeDtypeStruct((m, n), BF16), jax.ShapeDtypeStruct((m, LANES), F32)]
    if cast_rows is not None:
        w, prefix = cast_rows
        cols = w.shape[-1]
        row_block = lambda i, j: jnp.where(i == 0, j, nj - 1)
        in_specs.append(pl.BlockSpec((None,) * len(prefix) + (tn, cols),
                                     lambda i, j: prefix + (row_block(i, j), 0)))
        operands.append(w)
        out_specs.append(pl.BlockSpec((tn, cols), lambda i, j: (row_block(i, j), 0)))
        out_shape.append(jax.ShapeDtypeStruct((n, cols), BF16))
    return pl.pallas_call(
        body,
        grid=(m // tm, nj),
        in_specs=in_specs,
        out_specs=out_specs,
        out_shape=out_shape,
        compiler_params=_params("arbitrary", "arbitrary"),
        name=name,
    )(*operands)


def _attn_near_bias():
    il = np.arange(ATTN_TQ)[:, None]
    c = np.arange(ATTN_TQ + 2 * ATTN_NEAR_REACH)[None, :]
    d = il + ATTN_NEAR_REACH - c
    mult = np.zeros(d.shape, np.int64)
    for window, dil in ATTN_NEAR_CONFIGS:
        half = window // (2 * dil)
        mult += ((d % dil) == 0) & (np.abs(d) <= half * dil)
    return np.where(mult > 0, np.log2(np.maximum(mult, 1)), NEG_INF).astype(np.float32)


def _attn_far_bias():
    window, dil = ATTN_FAR_CONFIG
    t = np.arange(SEQ // dil)
    band = np.abs(t[:, None] - t[None, :]) <= window // (2 * dil)
    return np.where(band, 0.0, NEG_INF).astype(np.float32)


def _attn_body(q_ref, k_ref, v_ref, gq_ref, gk_ref, cos_ref, sin_ref, near_ref, far_ref, go_ref,
               o_ref, qs_ref, ks_ref, vs_ref, qf_ref, kf_ref, far_acc_ref, far_den_ref, far_max_ref):
    seq = q_ref.shape[1]
    hd = HEAD_DIM
    cos = cos_ref[...]
    sin = sin_ref[...]

    def norm_rot(x, g):
        y = x * lax.rsqrt(jnp.mean(x * x, axis=-1, keepdims=True) + RMS_EPS) * g
        return y * cos + pltpu.roll(y, hd // 2, axis=1) * sin

    q_scale = math.log2(math.e) / math.sqrt(hd)
    for x_ref, g_ref, scale, f32_ref, bf16_ref in ((q_ref, gq_ref, q_scale, qf_ref, qs_ref),
                                                   (k_ref, gk_ref, 1.0, kf_ref, ks_ref)):
        x = norm_rot(x_ref[0], g_ref[...]) * scale
        f32_ref[...] = x
        bf16_ref[...] = x.astype(BF16)
    vs_ref[:, 0:hd] = v_ref[0].astype(BF16)
    vs_ref[:, hd:] = jnp.ones((seq, LANES), BF16)

    def scores(q, k, bias):
        return lax.dot_general(q, k, (((1,), (1,)), ((), ())), preferred_element_type=F32) + bias

    def probabilities(s):
        m = jnp.max(s, axis=-1, keepdims=True)
        return jnp.exp2(s - m).astype(BF16), m

    dil = ATTN_FAR_CONFIG[1]
    members = seq // dil
    classes = [pl.ds(r, members, stride=dil) for r in range(dil)]
    far_bias = far_ref[...]
    s = jnp.concatenate([scores(qf_ref[cls, :].astype(BF16), kf_ref[cls, :].astype(BF16), far_bias)
                         for cls in classes], axis=0)
    p, m = probabilities(s)
    ones = jnp.ones((members, LANES), BF16)
    for r, cls in enumerate(classes):
        block = slice(r * members, (r + 1) * members)
        v_ones = jnp.concatenate([v_ref[0, cls, :].astype(BF16), ones], axis=1)
        acc = _dot(p[block], v_ones)
        far_acc_ref[cls, :] = acc[:, 0:hd]
        far_den_ref[cls, :] = acc[:, hd:]
        far_max_ref[cls, :] = jnp.broadcast_to(m[block], (members, LANES))

    for t in range(seq // ATTN_TQ):
        q0 = t * ATTN_TQ
        rows = slice(q0, q0 + ATTN_TQ)
        lo = max(0, q0 - ATTN_NEAR_REACH)
        hi = min(seq, q0 + ATTN_TQ + ATTN_NEAR_REACH)
        c0 = lo - (q0 - ATTN_NEAR_REACH)
        p, m_near = probabilities(scores(qs_ref[rows, :], ks_ref[lo:hi, :],
                                         near_ref[:, c0:c0 + (hi - lo)]))
        acc = _dot(p, vs_ref[lo:hi, :])
        acc, den = acc[:, 0:hd], acc[:, hd:]
        m_far = far_max_ref[rows, :]
        m = jnp.maximum(m_near, m_far)
        w_near = jnp.exp2(m_near - m)
        w_far = jnp.exp2(m_far - m)
        o = ((acc * w_near + far_acc_ref[rows, :] * w_far)
             / (den * w_near + far_den_ref[rows, :] * w_far))
        o = o * lax.rsqrt(jnp.mean(o * o, axis=-1, keepdims=True) + RMS_EPS) * go_ref[...]
        o_ref[0, rows, :] = o.astype(o_ref.dtype)


def _attention(proj, gq, gk, cos, sin_signed, g_out):
    b, s, _ = proj.shape
    hd = HEAD_DIM
    near = jnp.asarray(_attn_near_bias())
    far = jnp.asarray(_attn_far_bias())
    head = lambda off: pl.BlockSpec((1, s, hd), lambda i, h: (i, 0, off + h))
    const2 = lambda shape: pl.BlockSpec(shape, lambda i, h: (0, 0))
    return pl.pallas_call(
        _attn_body,
        grid=(b, N_HEADS),
        in_specs=[head(0), head(N_HEADS), head(2 * N_HEADS),
                  const2((1, hd)), const2((1, hd)),
                  const2((s, hd)), const2((s, hd)),
                  const2(near.shape), const2(far.shape),
                  pl.BlockSpec((1, hd), lambda i, h: (0, h))],
        out_specs=pl.BlockSpec((1, s, hd), lambda i, h: (i, 0, h)),
        out_shape=jax.ShapeDtypeStruct((b, s, ATTN_WIDTH), BF16),
        scratch_shapes=[pltpu.VMEM((s, hd), BF16)] * 2 + [pltpu.VMEM((s, hd + LANES), BF16)]
        + [pltpu.VMEM((s, hd), F32)] * 5,
        compiler_params=_params("parallel", "arbitrary"),
        name="attention",
    )(proj, proj, proj, gq.reshape(1, hd), gk.reshape(1, hd), cos, sin_signed, near, far,
      g_out.reshape(1, ATTN_WIDTH))


def _shortconv_eo(ue, uo, w_ref, b_ref):
    half, edge = ue.shape[0], 8
    w0, w1, w2, b = w_ref[0:1, :], w_ref[1:2, :], w_ref[2:3, :], b_ref[...]
    taps = lambda prev, cur, nxt: w0 * prev + w1 * cur + w2 * nxt + b
    uo_prev = pltpu.roll(uo, 1, axis=0)
    ue_next = pltpu.roll(ue, half - 1, axis=0)
    row = lax.broadcasted_iota(jnp.int32, (edge, ue.shape[1]), 0)
    lo, hi = slice(0, edge), slice(half - edge, half)
    even_top = taps(jnp.where(row == 0, 0.0, uo_prev[lo]), ue[lo], uo[lo])
    odd_bot = taps(ue[hi], uo[hi], jnp.where(row == edge - 1, 0.0, ue_next[hi]))
    even = jnp.concatenate([even_top, taps(uo_prev, ue, uo)[edge:]], axis=0)
    odd = jnp.concatenate([taps(ue, uo, ue_next)[:half - edge], odd_bot], axis=0)
    return even, odd


def _filt_hidden_body(emb_ref, w1_ref, b1_ref, w2_ref, b2_ref, fr_ref, o_ref):
    hp = lax.Precision.HIGHEST
    h = jnp.dot(emb_ref[...], w1_ref[...], precision=hp, preferred_element_type=F32)
    h = jnp.sin(fr_ref[0:1, :] * (h + b1_ref[...]))
    h = jnp.dot(h, w2_ref[...], precision=hp, preferred_element_type=F32)
    o_ref[...] = jnp.sin(fr_ref[1:2, :] * (h + b2_ref[...]))


def _split_bf16(x):
    high = x.astype(BF16)
    return high, (x - high.astype(F32)).astype(BF16)


def _filt_out_body(h_ref, w3_ref, t_ref, delta_ref, o_ref):
    h_hi, h_lo = _split_bf16(h_ref[...])
    w_hi, w_lo = _split_bf16(w3_ref[...])
    h = _dot(h_hi, w_hi) + (_dot(h_hi, w_lo) + _dot(h_lo, w_hi))
    o_ref[...] = h * jnp.exp(-t_ref[...] * delta_ref[...])


def _pad_to(a, shape):
    return jnp.pad(a, [(0, t - s) for s, t in zip(a.shape, shape)])


def _hyena_filters(w1, b1, w2, b2, freq, w3, tn=FILT_TN):
    seq = SEQ
    t = jnp.linspace(0.0, 1.0, seq, dtype=F32)[:, None]
    bands = (FILTER_EMB_DIM - 1) // 2
    f = jnp.linspace(1e-4, bands - 1, bands, dtype=F32)[None, :]
    wpos = (2.0 * math.pi) * jnp.arange(seq, dtype=F32)[:, None] / seq
    emb = jnp.concatenate([t, jnp.cos(f * wpos), -jnp.sin(f * wpos)], axis=-1)
    deltas = jnp.abs(jnp.linspace(math.log(DECAY_FAST) / DECAY_TARGET,
                                  math.log(DECAY_SLOW) / DECAY_TARGET, HYENA_WIDTH, dtype=F32))
    ncol = w3.shape[1]
    deltas = jnp.tile(deltas, ncol // HYENA_WIDTH).reshape(1, ncol)
    hid = LANES
    hidden = pl.pallas_call(
        _filt_hidden_body,
        out_shape=jax.ShapeDtypeStruct((seq, hid), F32),
        name="filt_hidden",
    )(_pad_to(emb, (seq, hid)), _pad_to(w1, (hid, hid)), _pad_to(b1[None], (1, hid)),
      _pad_to(w2, (hid, hid)), _pad_to(b2[None], (1, hid)), _pad_to(freq, (2, hid)))
    return pl.pallas_call(
        _filt_out_body,
        grid=(ncol // tn,),
        in_specs=[pl.BlockSpec((seq, hid), lambda j: (0, 0)),
                  pl.BlockSpec((hid, tn), lambda j: (0, j)),
                  pl.BlockSpec((seq, 1), lambda j: (0, 0)),
                  pl.BlockSpec((1, tn), lambda j: (0, j))],
        out_specs=pl.BlockSpec((seq, tn), lambda j: (0, j)),
        out_shape=jax.ShapeDtypeStruct((seq, ncol), F32),
        compiler_params=_params("parallel"),
        name="filt_out",
    )(hidden, _pad_to(w3, (hid, ncol)), t, deltas)


def _dft_matrices():
    h = DFT_HALF
    def table(t, n):
        ang = ((t[:, None] * n[None, :]) & (DFT_N - 1)).astype(F32) * (2.0 * math.pi / DFT_N)
        return jnp.cos(ang), jnp.sin(ang)
    def cos_sin(n):
        c_hi, s_hi = (x[:, None, :] for x in table(jnp.arange(0, h, DFT_SPLIT, dtype=jnp.int32), n))
        c_lo, s_lo = (x[None, :, :] for x in table(jnp.arange(DFT_SPLIT, dtype=jnp.int32), n))
        return ((c_hi * c_lo - s_hi * s_lo).reshape(h, -1),
                (s_hi * c_lo + c_hi * s_lo).reshape(h, -1))
    even = 2 * jnp.arange(h, dtype=jnp.int32)
    (ce, se), (co, so) = cos_sin(even), cos_sin(even + 1)
    se, so = -se, -so
    alt = jnp.where(jnp.arange(h) % 2 == 0, 1.0, -1.0).astype(F32)
    first = jnp.arange(h) == 0
    fwd = [ce, co, jnp.where(first[:, None], alt[None, :], se),
           jnp.where(first[:, None], -alt[None, :], so)]
    wgt = jnp.where(first, 1.0 / DFT_N, 2.0 / DFT_N)[None, :]
    inv = [ce.T * wgt, jnp.where(first[None, :], alt[:, None] / DFT_N, se.T * wgt),
           co.T * wgt, jnp.where(first[None, :], -alt[:, None] / DFT_N, so.T * wgt)]
    return jnp.stack(fwd).astype(BF16), jnp.stack(inv).astype(BF16)


def _rdft(mats_ref, ze, zo):
    ze = ze.astype(BF16)
    zo = zo.astype(BF16)
    return (_dot(mats_ref[0], ze), _dot(mats_ref[1], zo), _dot(mats_ref[2], ze),
            _dot(mats_ref[3], zo))


def _first_row(shape):
    return lax.broadcasted_iota(jnp.int32, shape, 0) == 0


def _seq_specs(operand, lead, tn, index):
    def spec(rows, width, offset):
        def index_map(*g):
            pre, j = index(*g)
            return pre + (0, offset + j * (tn // width))
        return pl.BlockSpec((None,) * lead + (rows, width), index_map)
    if not _is_natural(operand):
        return [spec(DFT_HALF, tn, 0)] * 2, list(operand)
    array, col = operand
    groups = tn // LANES
    return [spec(SEQ, LANES, col // LANES + g) for g in range(groups)], [array] * groups


def _load_seq(refs, natural):
    if not natural:
        return refs[0][...], refs[1][...]
    parity = lambda p: jnp.concatenate([r[pl.ds(p, DFT_HALF, stride=2), :] for r in refs], axis=1)
    return parity(0), parity(1)


def _kspec_body(mats_ref, *refs):
    h = DFT_HALF
    *h_refs, hb0_ref, kre_ref, kim_ref = refs
    n = len(h_refs) // 2
    fer, for_, fei, foi = _rdft(mats_ref, *_load_seq(h_refs[:n], True))
    ber, bor, bei, boi = _rdft(mats_ref, *_load_seq(h_refs[n:], True))
    hb0 = hb0_ref[0:1, :]
    kre_ref[0:h, :] = (fer + for_) + (ber + bor) - hb0
    kre_ref[h:, :] = (fer - for_) + (ber - bor) - hb0
    kim_ref[0:h, :] = (fei + foi) - (bei + boi)
    kim_ref[h:, :] = (foi - fei) - (boi - bei)
    r = 8
    first = _first_row((r, fer.shape[1]))
    kim_ref[0:r, :] = jnp.where(first, fei[:r] + bei[:r] - hb0, (fei + foi - bei - boi)[:r])
    kim_ref[h:h + r, :] = jnp.where(first, foi[:r] - boi[:r], (foi - fei - boi + bei)[:r])


def _kspec(fwd, filt, tn=KSPEC_TN):
    c = HYENA_WIDTH
    nc = c // tn
    out = jax.ShapeDtypeStruct((SEQ, HYENA_ORDER * c), F32)
    index = lambda o, j: ((), o * 2 * nc + j)
    hf_specs, hf_ops = _seq_specs((filt, 0), 0, tn, index)
    hb_specs, hb_ops = _seq_specs((filt, c), 0, tn, index)
    return pl.pallas_call(
        _kspec_body,
        grid=(HYENA_ORDER, nc),
        in_specs=[pl.BlockSpec(fwd.shape, lambda o, j: (0, 0, 0), pipeline_mode=pl.Buffered(1))]
        + hf_specs + hb_specs + [pl.BlockSpec((8, tn), lambda o, j: (0, o * 2 * nc + nc + j))],
        out_specs=[pl.BlockSpec((SEQ, tn), lambda o, j: (0, o * nc + j))] * 2,
        out_shape=[out, out],
        compiler_params=_params("parallel", "arbitrary"),
        name="kspec",
    )(fwd, *hf_ops, *hb_ops, filt)


def _dft_fwd_body(mats_ref, *refs, nz, natural, conv):
    ze, zo = _load_seq(refs[:nz], natural)
    if conv:
        ze, zo = _shortconv_eo(ze, zo, *refs[nz:nz + 2])
    kre_ref, kim_ref, yre_ref, yim_ref = refs[-4:]
    h = DFT_HALF
    er, or_, ei, oi = _rdft(mats_ref, ze, zo)
    out = yre_ref.dtype
    for rows, zr, zi in ((slice(0, h), er + or_, ei + oi), (slice(h, 2 * h), er - or_, oi - ei)):
        kr, ki = kre_ref[rows, :], kim_ref[rows, :]
        yre_ref[rows, :] = (zr * kr - zi * ki).astype(out)
        yim_ref[rows, :] = (zr * ki + zi * kr).astype(out)
    r = 16
    first = _first_row((r, er.shape[1]))
    e, o, p, q = er[:r], or_[:r], ei[:r], oi[:r]
    krt, krb, kit, kib = kre_ref[0:r, :], kre_ref[h:h + r, :], kim_ref[0:r, :], kim_ref[h:h + r, :]
    re4 = p * kit - q * kib
    im4 = p * kib + q * kit
    yre_ref[0:r, :] = jnp.where(first, (e + o) * krt, (e + o) * krt - (p + q) * kit).astype(out)
    yim_ref[0:r, :] = jnp.where(first, re4 + im4, (e + o) * kit + (p + q) * krt).astype(out)
    yre_ref[h:h + r, :] = jnp.where(first, (e - o) * krb, (e - o) * krb - (q - p) * kib).astype(out)
    yim_ref[h:h + r, :] = jnp.where(first, im4 - re4, (e - o) * kib + (q - p) * krb).astype(out)


def _is_natural(operand):
    return isinstance(operand[1], int)


def _dft_fwd(fwd, z, conv, kre, kim, order, tn=DFT_TN):
    b = z[0].shape[0]
    c = HYENA_WIDTH
    nc = c // tn
    out = jax.ShapeDtypeStruct((b, SEQ, c), BF16)
    z_specs, z_ops = _seq_specs(z, 1, tn, lambda j, i: ((i,), j))
    in_specs = [pl.BlockSpec(fwd.shape, lambda j, i: (0, 0, 0), pipeline_mode=pl.Buffered(1))]
    in_specs += z_specs
    operands = [fwd] + z_ops
    if conv is not None:
        cw, cb, col = conv
        in_specs += [pl.BlockSpec((cw.shape[0], tn), lambda j, i: (0, col // tn + j)),
                     pl.BlockSpec((1, tn), lambda j, i: (0, col // tn + j))]
        operands += [cw, cb]
    in_specs += [pl.BlockSpec((SEQ, tn), lambda j, i: (0, order * nc + j))] * 2
    operands += [kre, kim]
    return pl.pallas_call(
        functools.partial(_dft_fwd_body, nz=len(z_ops), natural=_is_natural(z),
                          conv=conv is not None),
        grid=(nc, b),
        in_specs=in_specs,
        out_specs=[pl.BlockSpec((None, SEQ, tn), lambda j, i: (i, 0, j))] * 2,
        out_shape=[out, out],
        compiler_params=_params("parallel", "arbitrary"),
        name="dft_fwd",
    )(*operands)


def _dft_inv_body(mats_ref, yre_ref, yim_ref, *refs, nz, z_natural, conv_z, ng, final):
    ze, zo = _load_seq(refs[:nz], z_natural)
    refs = refs[nz:]
    if conv_z:
        ze, zo = _shortconv_eo(ze, zo, *refs[:2])
        refs = refs[2:]
    ge, go = _shortconv_eo(*_load_seq(refs[:ng], True), *refs[ng:ng + 2])
    b_ref, gn_ref, *outs = refs[ng + 2:]
    h = DFT_HALF
    yrt, yrb = yre_ref[0:h, :].astype(F32), yre_ref[h:, :].astype(F32)
    yit, yib = yim_ref[0:h, :].astype(F32), yim_ref[h:, :].astype(F32)
    ye = _dot(mats_ref[0], (yrt + yrb).astype(BF16)) + _dot(mats_ref[1], (yit - yib).astype(BF16))
    yo = _dot(mats_ref[2], (yrt - yrb).astype(BF16)) + _dot(mats_ref[3], (yit + yib).astype(BF16))
    bias = b_ref[...]
    halves = (ge * (ye + bias * ze), go * (yo + bias * zo))
    if not final:
        for o_ref, zn in zip(outs, halves):
            o_ref[...] = zn
        return
    o_ref, stage_ref = outs
    for gi in range(halves[0].shape[1] // HYENA_GROUP):
        sl = slice(gi * HYENA_GROUP, (gi + 1) * HYENA_GROUP)
        for parity, zn in enumerate(halves):
            blk = zn[:, sl]
            blk = blk * lax.rsqrt(jnp.mean(blk * blk, axis=-1, keepdims=True) + RMS_EPS)
            stage_ref[pl.ds(parity, h, stride=2), :] = blk * gn_ref[:, sl]
        o_ref[:, sl] = stage_ref[...].astype(o_ref.dtype)


def _dft_inv(inv, yre, yim, z, conv_z, gate, conv_g, bias, gnorm, final, tn=DFT_TN):
    b = yre.shape[0]
    c = HYENA_WIDTH
    index = lambda j, i: ((i,), j)
    taps = lambda cw, col: [pl.BlockSpec((cw.shape[0], tn), lambda j, i: (0, col // tn + j)),
                            pl.BlockSpec((1, tn), lambda j, i: (0, col // tn + j))]
    z_specs, z_ops = _seq_specs(z, 1, tn, index)
    g_specs, g_ops = _seq_specs(gate, 1, tn, index)
    in_specs = [pl.BlockSpec(inv.shape, lambda j, i: (0, 0, 0), pipeline_mode=pl.Buffered(1)),
                pl.BlockSpec((None, SEQ, tn), lambda j, i: (i, 0, j)),
                pl.BlockSpec((None, SEQ, tn), lambda j, i: (i, 0, j))] + z_specs
    operands = [inv, yre, yim] + z_ops
    if conv_z is not None:
        in_specs += taps(conv_z[0], conv_z[2])
        operands += list(conv_z[:2])
    in_specs += g_specs + taps(conv_g[0], conv_g[2])
    in_specs += [pl.BlockSpec((1, tn), lambda j, i: (0, j))] * 2
    operands += g_ops + [conv_g[0], conv_g[1], bias.reshape(1, c), gnorm.reshape(1, c)]
    if final:
        out_specs = pl.BlockSpec((None, SEQ, tn), lambda j, i: (i, 0, j))
        out_shape = jax.ShapeDtypeStruct((b, SEQ, c), BF16)
        scratch = [pltpu.VMEM((SEQ, HYENA_GROUP), F32)]
    else:
        out_specs = [pl.BlockSpec((None, DFT_HALF, tn), lambda j, i: (i, 0, j))] * 2
        out_shape = [jax.ShapeDtypeStruct((b, DFT_HALF, c), F32)] * 2
        scratch = []
    return pl.pallas_call(
        functools.partial(_dft_inv_body, nz=len(z_ops), z_natural=_is_natural(z),
                          conv_z=conv_z is not None, ng=len(g_ops), final=final),
        grid=(c // tn, b),
        in_specs=in_specs,
        out_specs=out_specs,
        out_shape=out_shape,
        scratch_shapes=scratch,
        compiler_params=_params("parallel", "arbitrary"),
        name="dft_inv",
    )(*operands)


def _hyena(proj, dft, conv_w, conv_b, w1, b1, w2, b2, freq, w3, filt_bias, g_out):
    fwd, inv = dft
    c = HYENA_WIDTH
    raw = IN_COLS - (HYENA_ORDER + 1) * c
    conv_b = conv_b.reshape(1, -1)
    filt = _hyena_filters(w1, b1, w2, b2, freq, w3)
    kre, kim = _kspec(fwd, filt)
    z, conv_z = (proj, raw), (conv_w, conv_b, 0)
    for order in range(HYENA_ORDER):
        final = order == HYENA_ORDER - 1
        yre, yim = _dft_fwd(fwd, z, conv_z, kre, kim, order)
        z = _dft_inv(inv, yre, yim, z, conv_z, (proj, raw + (order + 1) * c),
                     (conv_w, conv_b, (order + 1) * c), filt_bias[order], g_out, final)
        conv_z = None
    return z


def _rotary_tables():
    half = HEAD_DIM // 2
    pos = jnp.arange(SEQ, dtype=F32)
    inv = ROPE_THETA ** (-jnp.arange(half, dtype=F32) / half)
    ang = pos[:, None] * inv[None, :]
    cos = jnp.cos(ang)
    sin = jnp.sin(ang)
    return jnp.concatenate([cos, cos], axis=-1), jnp.concatenate([-sin, sin], axis=-1)


def _mixer(proj, dft, rot, q_norm, k_norm, conv_w, conv_b, w1, b1, w2, b2, freq, w3,
           filt_bias, attn_out_norm, hyena_out_norm):
    attn = _attention(proj, q_norm, k_norm, rot[0], rot[1], attn_out_norm)
    hy = _hyena(proj, dft, conv_w, conv_b, w1, b1, w2, b2, freq, w3, filt_bias, hyena_out_norm)
    return attn, hy


def _residual_matmul(a_list, weight, x, scale, norm_next, tm, chunk, name, tn=MM_TN, a_buffers=1):
    width = weight[0].shape[-1]
    body = functools.partial(_mm_res_body, na=len(a_list), scale=scale,
                             norm_width=0 if norm_next is None else width,
                             chunk=chunk)
    x, *normed = _matmul(body, a_list, [weight], F32, tm, tn, name, res=x, norm_next=norm_next,
                         a_buffers=a_buffers)
    return x, normed


def _ffn(x, normed, w_gate, w_up, w_down, idx, norm_next):
    xg, rf = normed
    a, wd = _matmul(_swiglu_cast_body, [xg], [(w_gate, idx), (w_up, idx)], BF16, MM_TM, MM_TN,
                    "ffn_up", row_factor=rf, cast_rows=(w_down, idx))
    return _residual_matmul([a], (wd, ()), x, 0.5, norm_next, FFN_DOWN_TM, FFN_DOWN_TM, "ffn_down")


def kernel(x, ffn_norm, ffn_w_gate, ffn_w_up, ffn_w_down, mix_norm, w_in, q_norm, k_norm, conv_w, conv_b, filt_w1, filt_b1, filt_w2, filt_b2, filt_freq, filt_w3, filt_bias, attn_out_norm, hyena_out_norm, w_out):
    b, s, d = x.shape
    x = x.reshape(b * s, d)
    dft = _dft_matrices()
    rot = _rotary_tables()
    normed = _norm_prep(x, ffn_norm[0, 0])
    for l in range(DEPTH):
        x, normed = _ffn(x, normed, ffn_w_gate, ffn_w_up, ffn_w_down, (l, 0), mix_norm[l])
        proj, = _matmul(_mm_body, [normed[0]], [(w_in, (l,))], F32, MM_TM, IN_PROJ_TN, "in_proj",
                        row_factor=normed[1])
        attn, hy = _mixer(proj.reshape(b, s, IN_COLS), dft, rot, q_norm[l], k_norm[l], conv_w[l],
                          conv_b[l], filt_w1[l], filt_b1[l], filt_w2[l], filt_b2[l], filt_freq[l],
                          filt_w3[l], filt_bias[l], attn_out_norm[l], hyena_out_norm[l])
        x, normed = _residual_matmul([attn.reshape(b * s, -1), hy.reshape(b * s, -1)],
                                     (w_out, (l,)), x, 1.0, ffn_norm[l, 1], OUT_PROJ_TM, MM_CHUNK,
                                     "out_proj", tn=OUT_PROJ_TN, a_buffers=2)
        last = l == DEPTH - 1
        x, normed = _ffn(x, normed, ffn_w_gate, ffn_w_up, ffn_w_down, (l, 1),
                         None if last else ffn_norm[l + 1, 0])
    return x.reshape(b, s, d)
```
